```python
import jax
import jax.numpy as jnp
from jax import lax
import numpy as np

D_MODEL = 2048
BATCH = 4
SEQ = 4096
DEPTH = 4

GRID_W = 64
CTX_LEN = 256
EPS = 1e-6
N_BRANCH = 4
BR_W = 512
NA_HEADS = 4
NA_HEAD_DIM = BR_W // NA_HEADS
NA_ROWS = 8
NA_COLS = 16
POOL_WINDOWS = (2, 4, 8, 16)
POOL_GROUP = BR_W // len(POOL_WINDOWS)
MLA_HEADS = 4
MLA_NOPE = 128
MLA_ROPE = 64
MLA_V = BR_W // MLA_HEADS
MLA_Q_LORA = 512
MLA_KV_LORA = 512
ROPE_THETA = 10000.0
ATTN_BLOCK = 128
ML_HEADS = 4
ML_HEAD_DIM = BR_W // ML_HEADS
ML_CHUNK = 64
CONV_K = 4

IN_SPLITS = (
    ('a_qkv', 3 * BR_W), ('a_gate', BR_W),
    ('b_in', BR_W), ('b_gate', BR_W),
    ('c_q', MLA_Q_LORA), ('c_kv', MLA_KV_LORA), ('c_kr', MLA_ROPE), ('c_gate', BR_W),
    ('d_qkv', 3 * BR_W), ('d_o', BR_W), ('d_if', 4 * ML_HEADS), ('d_gate', BR_W),
    ('merge', N_BRANCH * D_MODEL),
)
IN_NAMES = tuple(name for name, _ in IN_SPLITS)
IN_OFFSETS = tuple(int(o) for o in np.cumsum([w for _, w in IN_SPLITS])[:-1])
D_IN = sum(w for _, w in IN_SPLITS)

kernel_name = 'hybrid_gated_branch_diffusion_trunk'

F32 = jnp.float32


def rmsnorm(x, g):
    xf = x.astype(F32)
    y = xf * lax.rsqrt(jnp.mean(xf * xf, axis=-1, keepdims=True) + EPS)
    return (y * g.astype(F32)).astype(x.dtype)


def split_heads(t, h):
    b, n, _ = t.shape
    return t.reshape(b, n, h, -1).transpose(0, 2, 1, 3)


def merge_heads(t):
    b, h, n, d = t.shape
    return t.transpose(0, 2, 1, 3).reshape(b, n, h * d)


def modulate_project(h, cond, norm_g, w_mod, b_mod, w_in):
    mod = jax.nn.silu(cond) @ w_mod + b_mod
    shift, scale, gate = jnp.split(mod, 3, axis=-1)
    hn = rmsnorm(h, norm_g) * (1 + scale) + shift
    parts = jnp.split(hn @ w_in, list(IN_OFFSETS), axis=-1)
    return dict(zip(IN_NAMES, parts)), gate


def dense_attention(q, k, v, scale):
    s = jnp.einsum('bhqd,bhkd->bhqk', q, k).astype(F32) * scale
    p = jax.nn.softmax(s, axis=-1).astype(v.dtype)
    return jnp.einsum('bhqk,bhkd->bhqd', p, v)


def blocked_joint_attention(q_a, q_b, k_a, v_a, k_b, v_b, scale):
    b, h, t, _ = q_a.shape
    nb = t // ATTN_BLOCK

    def to_blocks(a):
        return a.reshape(b, h, nb, ATTN_BLOCK, a.shape[-1]).transpose(2, 0, 1, 3, 4)

    def one_block(args):
        qa, qb = args
        s = jnp.concatenate([jnp.einsum('bhqd,bhkd->bhqk', qa, k_a),
                             jnp.einsum('bhqd,bhkd->bhqk', qb, k_b)], axis=-1).astype(F32) * scale
        p = jax.nn.softmax(s, axis=-1).astype(v_a.dtype)
        return (jnp.einsum('bhqk,bhkd->bhqd', p[..., :t], v_a)
                + jnp.einsum('bhqk,bhkd->bhqd', p[..., t:], v_b))

    out = lax.map(one_block, (to_blocks(q_a), to_blocks(q_b)))
    return out.transpose(1, 2, 0, 3, 4).reshape(b, h, t, -1)


def neighbourhood_attention(q, k, v, kc, vc, rpb):
    b, h, t, dh = q.shape
    rows = t // GRID_W
    kr = min(NA_ROWS, rows)
    scale = dh ** -0.5
    qg = q.reshape(b, h, rows, GRID_W, dh)
    r = jnp.arange(rows)
    ridx = jnp.clip(r - kr // 2, 0, rows - kr)[:, None] + jnp.arange(kr)[None, :]
    kb = jnp.take(k.reshape(b, h, rows, GRID_W, dh), ridx, axis=2)
    vb = jnp.take(v.reshape(b, h, rows, GRID_W, dh), ridx, axis=2)
    cq = jnp.arange(GRID_W)
    cstart = jnp.clip(cq - NA_COLS // 2, 0, GRID_W - NA_COLS)
    valid = (cq[None, :] >= cstart[:, None]) & (cq[None, :] < cstart[:, None] + NA_COLS)
    roff = ridx - r[:, None] + NA_ROWS - 1
    coff = jnp.clip(cq[None, :] - cq[:, None] + NA_COLS - 1, 0, 2 * NA_COLS - 2)
    bias = rpb[:, roff[:, None, :, None], coff[None, :, None, :]].astype(F32)
    s_loc = jnp.einsum('bhrqd,bhrkcd->bhrqkc', qg, kb).astype(F32) * scale + bias[None]
    s_loc = jnp.where(valid[:, None, :], s_loc, -jnp.inf)
    s_ctx = jnp.einsum('bhrqd,bhnd->bhrqn', qg, kc).astype(F32) * scale
    n_loc = kr * GRID_W
    p = jax.nn.softmax(jnp.concatenate([s_loc.reshape(b, h, rows, GRID_W, n_loc), s_ctx], axis=-1),
                       axis=-1).astype(v.dtype)
    out = (jnp.einsum('bhrqkc,bhrkcd->bhrqd', p[..., :n_loc].reshape(b, h, rows, GRID_W, kr, GRID_W), vb)
           + jnp.einsum('bhrqn,bhnd->bhrqd', p[..., n_loc:], vc))
    return out.reshape(b, h, t, dh)


def mixer_neighbourhood(pl, pc, rpb, need_ctx):
    qc, kc, vc = (split_heads(a, NA_HEADS) for a in jnp.split(pc['a_qkv'], 3, axis=-1))
    q, k, v = (split_heads(a, NA_HEADS) for a in jnp.split(pl['a_qkv'], 3, axis=-1))
    y = merge_heads(neighbourhood_attention(q, k, v, kc, vc, rpb)) * jax.nn.silu(pl['a_gate'])
    y_c = None
    if need_ctx:
        y_c = merge_heads(dense_attention(qc, kc, vc, NA_HEAD_DIM ** -0.5)) * jax.nn.silu(pc['a_gate'])
    return y, y_c


def multiscale_pool_diff(u):
    b, n, ch = u.shape
    uf = u.astype(F32)
    csum = jnp.concatenate([jnp.zeros((b, 1, ch), F32), jnp.cumsum(uf, axis=1)], axis=1)
    t = jnp.arange(n)
    outs = []
    for gi, w in enumerate(POOL_WINDOWS):
        sl = slice(gi * POOL_GROUP, (gi + 1) * POOL_GROUP)
        lo = jnp.maximum(t - w // 2, 0)
        hi = jnp.minimum(t + (w - 1 - w // 2), n - 1)
        cnt = (hi - lo + 1).astype(F32)[:, None]
        mean = (csum[:, hi + 1, sl] - csum[:, lo, sl]) / cnt
        outs.append(mean - uf[:, :, sl])
    return jnp.concatenate(outs, axis=-1)


def pool_branch(p, pool_w, pool_scale):
    u = p['b_in']
    b, n, _ = u.shape
    d = multiscale_pool_diff(u).reshape(b, n, len(POOL_WINDOWS), POOL_GROUP)
    y = jnp.einsum('bngi,gio->bngo', d, pool_w.astype(F32)).reshape(b, n, BR_W)
    y = (y * pool_scale.astype(F32)).astype(u.dtype)
    return y * jax.nn.silu(p['b_gate'])


def mixer_pool(pl, pc, pool_w, pool_scale, need_ctx):
    y = pool_branch(pl, pool_w, pool_scale)
    y_c = pool_branch(pc, pool_w, pool_scale) if need_ctx else None
    return y, y_c


def rope_axis(x, pos):
    n = x.shape[-1] // 2
    inv = ROPE_THETA ** (-jnp.arange(n, dtype=F32) / n)
    ang = pos.astype(F32)[:, None] * inv
    cos, sin = jnp.cos(ang), jnp.sin(ang)
    x1, x2 = x[..., :n].astype(F32), x[..., n:].astype(F32)
    return jnp.concatenate([x1 * cos - x2 * sin, x1 * sin + x2 * cos], axis=-1)


def axial_rope(x, rows, cols):
    half = x.shape[-1] // 2
    out = jnp.concatenate([rope_axis(x[..., :half], rows), rope_axis(x[..., half:], cols)], axis=-1)
    return out.astype(x.dtype)


def mla_project(p, gq, gkv, w_uq, w_ukv):
    b, n, _ = p['c_q'].shape
    q = (rmsnorm(p['c_q'], gq) @ w_uq).reshape(b, n, MLA_HEADS, MLA_NOPE + MLA_ROPE).transpose(0, 2, 1, 3)
    kv = (rmsnorm(p['c_kv'], gkv) @ w_ukv).reshape(b, n, MLA_HEADS, MLA_NOPE + MLA_V).transpose(0, 2, 1, 3)
    k_rope = p['c_kr'][:, None]
    return q[..., :MLA_NOPE], q[..., MLA_NOPE:], kv[..., :MLA_NOPE], kv[..., MLA_NOPE:], k_rope


def mixer_mla(pl, pc, gq, gkv, w_uq, w_ukv, need_ctx):
    scale = (MLA_NOPE + MLA_ROPE) ** -0.5
    qn_c, qr_c, kn_c, v_c, kr_c = mla_project(pc, gq, gkv, w_uq, w_ukv)
    k_ctx = jnp.concatenate([kn_c, jnp.broadcast_to(kr_c, kn_c.shape[:-1] + (MLA_ROPE,))], axis=-1)
    qn, qr, kn, v, kr = mla_project(pl, gq, gkv, w_uq, w_ukv)
    t = qn.shape[2]
    pos = jnp.arange(t)
    rows, cols = pos // GRID_W, pos % GRID_W
    q_lat = jnp.concatenate([qn, axial_rope(qr, rows, cols)], axis=-1)
    q_for_ctx = jnp.concatenate([qn, qr], axis=-1)
    k_lat = jnp.concatenate([kn, jnp.broadcast_to(axial_rope(kr, rows, cols), kn.shape[:-1] + (MLA_ROPE,))], axis=-1)
    y = merge_heads(blocked_joint_attention(q_lat, q_for_ctx, k_lat, v, k_ctx, v_c, scale)) * jax.nn.silu(pl['c_gate'])
    y_c = None
    if need_ctx:
        q_c = jnp.concatenate([qn_c, qr_c], axis=-1)
        y_c = merge_heads(dense_attention(q_c, k_ctx, v_c, scale)) * jax.nn.silu(pc['c_gate'])
    return y, y_c


def short_conv(x, w):
    n = x.shape[1]
    xp = jnp.pad(x, ((0, 0), (CONV_K // 2, CONV_K - 1 - CONV_K // 2), (0, 0)))
    acc = xp[:, 0:n] * w[0]
    for j in range(1, CONV_K):
        acc = acc + xp[:, j:j + n] * w[j]
    return acc


def mlstm_inputs(p, conv_w, b_if):
    b, n, _ = p['d_qkv'].shape
    q_raw, k_raw, v = jnp.split(p['d_qkv'], 3, axis=-1)
    q, k = jnp.split(jax.nn.silu(short_conv(jnp.concatenate([q_raw, k_raw], axis=-1), conv_w)), 2, axis=-1)
    q = split_heads(q, ML_HEADS).astype(F32)
    k = split_heads(k, ML_HEADS).astype(F32) * (ML_HEAD_DIM ** -0.5)
    v = split_heads(v, ML_HEADS).astype(F32)
    g = (p['d_if'].astype(F32) + b_if.astype(F32)).reshape(b, n, 2, 2, ML_HEADS).transpose(2, 3, 0, 4, 1)
    li = g[:, 0]
    lf = jax.nn.log_sigmoid(g[:, 1])
    return q, k, v, li, lf


def mlstm_scan(q, k, v, li, lf, state):
    b, h, n, _ = q.shape
    nc = n // ML_CHUNK
    tril = jnp.tril(jnp.ones((ML_CHUNK, ML_CHUNK), dtype=bool))

    def chunks(a):
        return jnp.moveaxis(a.reshape(a.shape[:2] + (nc, ML_CHUNK) + a.shape[3:]), 2, 0)

    def step(carry, xs):
        C, nv, m = carry
        qc, kc, vc, lic, lfc = xs
        bc = jnp.cumsum(lfc, axis=-1)
        dmat = jnp.where(tril, bc[..., :, None] - bc[..., None, :] + lic[..., None, :], -jnp.inf)
        inter = bc + m[..., None]
        m_t = jnp.maximum(inter, jnp.max(dmat, axis=-1))
        s = jnp.einsum('bhtd,bhsd->bhts', qc, kc) * jnp.exp(dmat - m_t[..., None])
        a = jnp.exp(inter - m_t)
        num = a[..., None] * jnp.einsum('bhtd,bhde->bhte', qc, C) + jnp.einsum('bhts,bhse->bhte', s, vc)
        den = a * jnp.einsum('bhtd,bhd->bht', qc, nv) + jnp.sum(s, axis=-1)
        h_out = num / jnp.maximum(jnp.abs(den), jnp.exp(-m_t))[..., None]
        b_last = bc[..., -1]
        gk = b_last[..., None] - bc + lic
        m_new = jnp.maximum(b_last + m, jnp.max(gk, axis=-1))
        wk = jnp.exp(gk - m_new[..., None])
        decay = jnp.exp(b_last + m - m_new)
        C_new = decay[..., None, None] * C + jnp.einsum('bhsd,bhse->bhde', kc * wk[..., None], vc)
        n_new = decay[..., None] * nv + jnp.einsum('bhs,bhsd->bhd', wk, kc)
        return (C_new, n_new, m_new), h_out

    state, hs = lax.scan(step, state, tuple(chunks(a) for a in (q, k, v, li, lf)))
    return jnp.moveaxis(hs, 0, 2).reshape(b, h, n, -1), state


def mlstm_bidirectional(q, k, v, li, lf, states):
    h_f, s_f = mlstm_scan(q, k, v, li[0], lf[0], states[0])
    flip = lambda a: jnp.flip(a, axis=2)
    h_b, s_b = mlstm_scan(flip(q), flip(k), flip(v), flip(li[1]), flip(lf[1]), states[1])
    return h_f + flip(h_b), (s_f, s_b)


def mlstm_output(h, p, ml_gnorm):
    hn = h * lax.rsqrt(jnp.mean(h * h, axis=-1, keepdims=True) + EPS)
    y = merge_heads(hn) * ml_gnorm.astype(F32) * jax.nn.sigmoid(p['d_o'].astype(F32))
    return y.astype(p['d_gate'].dtype) * jax.nn.silu(p['d_gate'])


def mixer_mlstm(pl, pc, conv_w, b_if, ml_gnorm, need_ctx):
    q_c, k_c, v_c, li_c, lf_c = mlstm_inputs(pc, conv_w, b_if)
    b = q_c.shape[0]
    zero = (jnp.zeros((b, ML_HEADS, ML_HEAD_DIM, ML_HEAD_DIM), F32),
            jnp.zeros((b, ML_HEADS, ML_HEAD_DIM), F32),
            jnp.zeros((b, ML_HEADS), F32))
    h_c, ctx_states = mlstm_bidirectional(q_c, k_c, v_c, li_c, lf_c, (zero, zero))
    q, k, v, li, lf = mlstm_inputs(pl, conv_w, b_if)
    h, _ = mlstm_bidirectional(q, k, v, li, lf, ctx_states)
    y = mlstm_output(h, pl, ml_gnorm)
    y_c = mlstm_output(h_c, pc, ml_gnorm) if need_ctx else None
    return y, y_c


def merge_branches(merge_pre, ys, w_br, w_out):
    b, n, _ = merge_pre.shape
    g = jax.nn.sigmoid(merge_pre.reshape(b, n, N_BRANCH, D_MODEL))
    acc = g[:, :, 0] * (ys[0] @ w_br[0])
    for i in range(1, N_BRANCH):
        acc = acc + g[:, :, i] * (ys[i] @ w_br[i])
    return acc @ w_out


def hybrid_layer(x, xc, c, c_ctx, norm_g, w_mod, b_mod, w_in, na_rpb, pool_w, pool_scale,
                 mla_gq, mla_gkv, w_uq, w_ukv, conv_w, b_if, ml_gnorm, w_br, w_out, need_ctx):
    pl, gate = modulate_project(x, c[:, None, :], norm_g, w_mod, b_mod, w_in)
    pc, gate_c = modulate_project(xc, c_ctx[None, None, :], norm_g, w_mod, b_mod, w_in)
    ya, ya_c = mixer_neighbourhood(pl, pc, na_rpb, need_ctx)
    yb, yb_c = mixer_pool(pl, pc, pool_w, pool_scale, need_ctx)
    ym, ym_c = mixer_mla(pl, pc, mla_gq, mla_gkv, w_uq, w_ukv, need_ctx)
    yd, yd_c = mixer_mlstm(pl, pc, conv_w, b_if, ml_gnorm, need_ctx)
    x = x + gate * merge_branches(pl['merge'], (ya, yb, ym, yd), w_br, w_out)
    if need_ctx:
        xc = xc + gate_c * merge_branches(pc['merge'], (ya_c, yb_c, ym_c, yd_c), w_br, w_out)
    return x, xc


def setup_inputs(seed: int = 0) -> dict:
    key = jax.random.key(seed)
    ks = jax.random.split(key, 24)
    L = DEPTH

    def nrm(k, shape, s):
        return jax.random.normal(k, shape, F32) * s

    b_i = nrm(ks[8], (L, 2, 1, ML_HEADS), 0.1)
    b_f = jax.random.uniform(ks[9], (L, 2, 1, ML_HEADS), F32, 3.0, 6.0)
    return {
        'x': nrm(ks[0], (BATCH, SEQ, D_MODEL), 1.0),
        'c': nrm(ks[1], (BATCH, D_MODEL), 1.0),
        'ctx': nrm(ks[2], (BATCH, CTX_LEN, D_MODEL), 1.0),
        'c_ctx': nrm(ks[3], (D_MODEL,), 1.0),
        'norm_g': 1.0 + nrm(ks[4], (L, D_MODEL), 0.1),
        'w_mod': nrm(ks[5], (L, D_MODEL, 3 * D_MODEL), 0.5 * D_MODEL ** -0.5),
        'b_mod': nrm(ks[6], (L, 3 * D_MODEL), 0.02),
        'w_in': nrm(ks[7], (L, D_MODEL, D_IN), D_MODEL ** -0.5),
        'na_rpb': nrm(ks[10], (L, NA_HEADS, 2 * NA_ROWS - 1, 2 * NA_COLS - 1), 0.5),
        'pool_w': nrm(ks[11], (L, len(POOL_WINDOWS), POOL_GROUP, POOL_GROUP), POOL_GROUP ** -0.5),
        'pool_scale': 1.0 + nrm(ks[12], (L, BR_W), 0.1),
        'mla_gq': 1.0 + nrm(ks[13], (L, MLA_Q_LORA), 0.1),
        'mla_gkv': 1.0 + nrm(ks[14], (L, MLA_KV_LORA), 0.1),
        'w_uq': nrm(ks[15], (L, MLA_Q_LORA, MLA_HEADS * (MLA_NOPE + MLA_ROPE)), MLA_Q_LORA ** -0.5),
        'w_ukv': nrm(ks[16], (L, MLA_KV_LORA, MLA_HEADS * (MLA_NOPE + MLA_V)), MLA_KV_LORA ** -0.5),
        'conv_w': nrm(ks[17], (L, CONV_K, 2 * BR_W), CONV_K ** -0.5),
        'b_if': jnp.concatenate([b_i, b_f], axis=2).reshape(L, 4 * ML_HEADS),
        'ml_gnorm': 1.0 + nrm(ks[18], (L, BR_W), 0.1),
        'w_br': nrm(ks[19], (L, N_BRANCH, BR_W, D_MODEL), BR_W ** -0.5),
        'w_out': nrm(ks[20], (L, D_MODEL, D_MODEL), D_MODEL ** -0.5),
        'g_final': 1.0 + nrm(ks[21], (D_MODEL,), 0.1),
    }


def reference(x, c, ctx, c_ctx, norm_g, w_mod, b_mod, w_in, na_rpb, pool_w, pool_scale,
              mla_gq, mla_gkv, w_uq, w_ukv, conv_w, b_if, ml_gnorm, w_br, w_out, g_final):
    xc = ctx
    for l in range(DEPTH):
        x, xc = hybrid_layer(x, xc, c, c_ctx, norm_g[l], w_mod[l], b_mod[l], w_in[l], na_rpb[l],
                             pool_w[l], pool_scale[l], mla_gq[l], mla_gkv[l], w_uq[l], w_ukv[l],
                             conv_w[l], b_if[l], ml_gnorm[l], w_br[l], w_out[l],
                             need_ctx=(l < DEPTH - 1))
    return rmsnorm(x, g_final)
```

```python
import functools

import numpy as np
import jax
import jax.numpy as jnp
from jax import lax
from jax.experimental import pallas as pl
from jax.experimental.pallas import tpu as pltpu

F32 = jnp.float32
CDT = jnp.bfloat16

GRID_W = 64
EPS = 1e-6
N_BRANCH = 4
BR_W = 512
HEADS = 4
HEAD_DIM = BR_W // HEADS
NA_ROWS = 8
NA_COLS = 16
POOL_WINDOWS = (2, 4, 8, 16)
MLA_NOPE = 128
MLA_ROPE = 64
MLA_Q_LORA = 512
MLA_KV_LORA = 512
ROPE_THETA = 10000.0
CONV_K = 4

LANE = 128
VMEM_LIMIT = 56 * 1024 * 1024
NEG = -1e30

BLK_A_Q, BLK_A_K, BLK_A_V, BLK_A_GATE = 0, 4, 8, 12
BLK_B_IN, BLK_B_GATE = 16, 20
BLK_C_Q, BLK_C_KV, BLK_C_GATE = 24, 28, 32
BLK_D_Q, BLK_D_K, BLK_D_V, BLK_D_O, BLK_D_GATE = 36, 40, 44, 48, 52
BLK_C_KR = 56
BLK_D_LI, BLK_D_LF = 58, 59
N1_BLOCKS = 60
N1 = N1_BLOCKS * LANE

NA_QROWS = 8
NA_KROWS = 16
ML_CHUNK = 256


def _cparams(sem):
    return pltpu.CompilerParams(dimension_semantics=sem, vmem_limit_bytes=VMEM_LIMIT)


def _silu(x):
    return x * jax.nn.sigmoid(x)


def _pick(n, pref):
    t = min(n, pref)
    while n % t:
        t //= 2
    return t


def _mod_kernel(c_ref, w_ref, b_ref, o_ref):
    c = _silu(c_ref[...]).astype(CDT)
    o_ref[0] = jnp.dot(c, w_ref[0].astype(CDT), preferred_element_type=F32) + b_ref[0]


def _modulation(cond, w_mod, b_mod):
    nl, d, d3 = w_mod.shape
    r = cond.shape[0]
    tn = _pick(d3, 1024)
    return pl.pallas_call(
        _mod_kernel,
        grid=(nl, d3 // tn),
        in_specs=[pl.BlockSpec((r, d), lambda l, j: (0, 0)),
                  pl.BlockSpec((1, d, tn), lambda l, j: (l, 0, j)),
                  pl.BlockSpec((1, 1, tn), lambda l, j: (l, 0, j))],
        out_specs=pl.BlockSpec((1, r, tn), lambda l, j: (l, 0, j)),
        out_shape=jax.ShapeDtypeStruct((nl, r, d3), F32),
        compiler_params=_cparams(("parallel", "parallel")),
        name="modulation",
    )(cond, w_mod, b_mod.reshape(nl, 1, d3))


def _inproj_kernel(x_ref, mod_ref, g_ref, w_ref, o_ref, *rest, n_f32_cols):
    hn_ref = rest[-1]
    j = pl.program_id(1)

    @pl.when(j == 0)
    def _():
        x = x_ref[...]
        y = x * lax.rsqrt(jnp.mean(x * x, axis=-1, keepdims=True) + EPS) * g_ref[...]
        hn = y * (1.0 + mod_ref[0, 1:2, :]) + mod_ref[0, 0:1, :]
        hn_ref[...] = hn.astype(CDT)

    acc = jnp.dot(hn_ref[...], w_ref[...], preferred_element_type=F32)
    o_ref[...] = acc.astype(o_ref.dtype)
    if n_f32_cols:
        f_ref = rest[0]

        @pl.when(j == pl.num_programs(1) - 1)
        def _():
            f_ref[...] = acc[:, acc.shape[1] - n_f32_cols:]


def _inproj(x2, mod, mod_row, norm_g, w, tm, tn, n_f32_cols):
    t, d = x2.shape
    n = w.shape[1]
    out_shape = [jax.ShapeDtypeStruct((t, n), CDT)]
    out_specs = [pl.BlockSpec((tm, tn), lambda i, j: (i, j))]
    if n_f32_cols:
        out_shape.append(jax.ShapeDtypeStruct((t, n_f32_cols), F32))
        out_specs.append(pl.BlockSpec((tm, n_f32_cols), lambda i, j: (i, 0)))
    res = pl.pallas_call(
        functools.partial(_inproj_kernel, n_f32_cols=n_f32_cols),
        grid=(t // tm, n // tn),
        in_specs=[pl.BlockSpec((tm, d), lambda i, j: (i, 0)),
                  pl.BlockSpec((1, 3, d), lambda i, j: (mod_row(i), 0, 0)),
                  pl.BlockSpec((1, d), lambda i, j: (0, 0)),
                  pl.BlockSpec((d, tn), lambda i, j: (0, j))],
        out_specs=out_specs,
        out_shape=out_shape,
        scratch_shapes=[pltpu.VMEM((tm, d), CDT)],
        compiler_params=_cparams(("parallel", "arbitrary")),
        name="inproj",
    )(x2, mod, norm_g.reshape(1, d), w)
    return res


def _softmax_pv(parts, scale_out=None):
    m = None
    for s, _ in parts:
        mi = jnp.max(s, axis=-1, keepdims=True)
        m = mi if m is None else jnp.maximum(m, mi)
    l = None
    o = None
    for s, v in parts:
        p = jnp.exp(s - m)
        li = jnp.sum(p, axis=-1, keepdims=True)
        oi = jnp.dot(p.astype(CDT), v, preferred_element_type=F32)
        l = li if l is None else l + li
        o = oi if o is None else o + oi
    return o / l


def _qk(q, k):
    return lax.dot_general(q, k, (((1,), (1,)), ((), ())), preferred_element_type=F32)


def _na_kernel(q_ref, k_ref, v_ref, kc_ref, vc_ref, g_ref, bias_ref, o_ref, *, rows, scale):
    j = pl.program_id(2)
    start = jnp.clip(j * NA_QROWS - NA_ROWS // 2, 0, rows - NA_KROWS) * GRID_W
    start = pl.multiple_of(start, GRID_W)
    kw = k_ref[0, pl.ds(start, NA_KROWS * GRID_W), :]
    vw = v_ref[0, pl.ds(start, NA_KROWS * GRID_W), :]
    q = q_ref[0]
    s_loc = _qk(q, kw) * scale + bias_ref[0, 0]
    s_ctx = _qk(q, kc_ref[0]) * scale
    o = _softmax_pv([(s_loc, vw), (s_ctx, vc_ref[0])])
    o_ref[0] = (o * _silu(g_ref[0].astype(F32))).astype(o_ref.dtype)


def _na_bias_tables(rpb, rows):
    nj = rows // NA_QROWS
    w = GRID_W
    cq = np.arange(w)
    cstart = np.clip(cq - NA_COLS // 2, 0, w - NA_COLS)
    col_ok = (cq[None, :] >= cstart[:, None]) & (cq[None, :] < cstart[:, None] + NA_COLS)
    coff = np.clip(cq[None, :] - cq[:, None] + NA_COLS - 1, 0, 2 * NA_COLS - 2)
    tabs = []
    for jt in (0, min(1, nj - 1), nj - 1):
        start = int(np.clip(jt * NA_QROWS - NA_ROWS // 2, 0, rows - NA_KROWS))
        r = jt * NA_QROWS + np.arange(NA_QROWS)
        keyrow = start + np.arange(NA_KROWS)
        rs = np.clip(r - NA_ROWS // 2, 0, rows - NA_ROWS)
        row_ok = (keyrow[None, :] >= rs[:, None]) & (keyrow[None, :] < rs[:, None] + NA_ROWS)
        roff = np.clip(keyrow[None, :] - r[:, None] + NA_ROWS - 1, 0, 2 * NA_ROWS - 2)
        ok = row_ok[:, None, :, None] & col_ok[None, :, None, :]
        b = rpb[:, :, roff[:, None, :, None], coff[None, :, None, :]].astype(F32)
        b = jnp.where(ok[None, None], b, NEG)
        tabs.append(b.reshape(b.shape[0], b.shape[1], NA_QROWS * w, NA_KROWS * w))
    return jnp.stack(tabs, axis=1)


def _neighbourhood(p1, p1c, bias, bsz, n, nc):
    rows = n // GRID_W
    nj = rows // NA_QROWS
    tq = NA_QROWS * GRID_W

    def btype(j):
        return jnp.where(j == 0, 0, jnp.where(j == nj - 1, 2, 1))

    return pl.pallas_call(
        functools.partial(_na_kernel, rows=rows, scale=HEAD_DIM ** -0.5),
        grid=(bsz, HEADS, nj),
        in_specs=[pl.BlockSpec((1, tq, LANE), lambda b, h, j: (b, j, BLK_A_Q + h)),
                  pl.BlockSpec((1, n, LANE), lambda b, h, j: (b, 0, BLK_A_K + h)),
                  pl.BlockSpec((1, n, LANE), lambda b, h, j: (b, 0, BLK_A_V + h)),
                  pl.BlockSpec((1, nc, LANE), lambda b, h, j: (b, 0, BLK_A_K + h)),
                  pl.BlockSpec((1, nc, LANE), lambda b, h, j: (b, 0, BLK_A_V + h)),
                  pl.BlockSpec((1, tq, LANE), lambda b, h, j: (b, j, BLK_A_GATE + h)),
                  pl.BlockSpec((1, 1, tq, NA_KROWS * GRID_W), lambda b, h, j: (btype(j), h, 0, 0))],
        out_specs=pl.BlockSpec((1, tq, LANE), lambda b, h, j: (b, j, h)),
        out_shape=jax.ShapeDtypeStruct((bsz, n, BR_W), CDT),
        compiler_params=_cparams(("parallel", "parallel", "arbitrary")),
        name="neighbourhood_attention",
    )(p1, p1, p1, p1c, p1c, p1, bias)


def _ctx_attn_kernel(q_ref, k_ref, v_ref, g_ref, o_ref, *, scale):
    s = _qk(q_ref[0], k_ref[0]) * scale
    o = _softmax_pv([(s, v_ref[0])])
    o_ref[0] = (o * _silu(g_ref[0].astype(F32))).astype(o_ref.dtype)


def _ctx_attention(q, qblk, qw, k, kblk, kw, v, vblk, gate, gblk, scale):
    bsz, nc, _ = q.shape
    return pl.pallas_call(
        functools.partial(_ctx_attn_kernel, scale=scale),
        grid=(bsz, HEADS),
        in_specs=[pl.BlockSpec((1, nc, qw), lambda b, h: (b, 0, qblk + h)),
                  pl.BlockSpec((1, nc, kw), lambda b, h: (b, 0, kblk + h)),
                  pl.BlockSpec((1, nc, LANE), lambda b, h: (b, 0, vblk + h)),
                  pl.BlockSpec((1, nc, LANE), lambda b, h: (b, 0, gblk + h))],
        out_specs=pl.BlockSpec((1, nc, LANE), lambda b, h: (b, 0, h)),
        out_shape=jax.ShapeDtypeStruct((bsz, nc, BR_W), CDT),
        compiler_params=_cparams(("parallel", "parallel")),
        name="context_attention",
    )(q, k, v, gate)


POOL_PAD = 16
POOL_HALO = 8


def _pool_kernel(u_ref, g_ref, w_ref, sc_ref, o_ref, buf_a, buf_b, *, n):
    gi = pl.program_id(1)
    x = u_ref[0].astype(F32)
    t = lax.broadcasted_iota(jnp.int32, (n, LANE), 0)
    ext = n + 2 * POOL_HALO

    def finish(wsum, w):
        lo = jnp.maximum(t - w // 2, 0)
        hi = jnp.minimum(t + (w - 1 - w // 2), n - 1)
        cnt = (hi - lo + 1).astype(F32)
        d = wsum / cnt - x
        y = jnp.dot(d.astype(CDT), w_ref[0], preferred_element_type=F32) * sc_ref[...]
        o_ref[0] = (y * _silu(g_ref[0].astype(F32))).astype(o_ref.dtype)

    def window(ref, off):
        return ref[pl.ds(POOL_PAD + off, n), :]

    def wide(ref, off):
        return ref[pl.ds(POOL_PAD - POOL_HALO + off, ext), :]

    for buf in (buf_a, buf_b):
        buf[pl.ds(0, POOL_PAD), :] = jnp.zeros((POOL_PAD, LANE), F32)
        buf[pl.ds(POOL_PAD + n, POOL_PAD), :] = jnp.zeros((POOL_PAD, LANE), F32)
    buf_a[pl.ds(POOL_PAD, n), :] = x

    for level, w in enumerate(POOL_WINDOWS):
        @pl.when(gi == level)
        def _(level=level, w=w):
            if level == 0:
                finish(window(buf_a, -1) + window(buf_a, 0), w)
                return
            buf_b[pl.ds(POOL_PAD - POOL_HALO, ext), :] = wide(buf_a, -1) + wide(buf_a, 0)
            src, dst = buf_b, buf_a
            for lv in range(1, level):
                h = 1 << (lv - 1)
                dst[pl.ds(POOL_PAD - POOL_HALO, ext), :] = wide(src, -h) + wide(src, h)
                src, dst = dst, src
            h = 1 << (level - 1)
            finish(window(src, -h) + window(src, h), w)


def _pool(p1, pool_w, pool_scale):
    bsz, n, _ = p1.shape
    ng = len(POOL_WINDOWS)
    return pl.pallas_call(
        functools.partial(_pool_kernel, n=n),
        grid=(bsz, ng),
        in_specs=[pl.BlockSpec((1, n, LANE), lambda b, g: (b, 0, BLK_B_IN + g)),
                  pl.BlockSpec((1, n, LANE), lambda b, g: (b, 0, BLK_B_GATE + g)),
                  pl.BlockSpec((1, LANE, LANE), lambda b, g: (g, 0, 0)),
                  pl.BlockSpec((1, LANE), lambda b, g: (0, g))],
        out_specs=pl.BlockSpec((1, n, LANE), lambda b, g: (b, 0, g)),
        out_shape=jax.ShapeDtypeStruct((bsz, n, BR_W), CDT),
        scratch_shapes=[pltpu.VMEM((n + 2 * POOL_PAD, LANE), F32),
                        pltpu.VMEM((n + 2 * POOL_PAD, LANE), F32)],
        compiler_params=_cparams(("parallel", "arbitrary")),
        name="pool_branch",
    )(p1, p1, pool_w, pool_scale.reshape(1, BR_W))


def _rms(x, g):
    return x * lax.rsqrt(jnp.mean(x * x, axis=-1, keepdims=True) + EPS) * g


def _mla_proj_kernel(cq_ref, ckv_ref, kr_ref, gq_ref, gkv_ref, wq_ref, wkv_ref,
                     tq_c, tq_s, tk_c, tk_s, q_ref, k_ref, v_ref, *, scale):
    hq = _rms(cq_ref[...].astype(F32), gq_ref[...]).astype(CDT)
    hkv = _rms(ckv_ref[...].astype(F32), gkv_ref[...]).astype(CDT)
    q = jnp.dot(hq, wq_ref[...], preferred_element_type=F32) * scale
    kv = jnp.dot(hkv, wkv_ref[...], preferred_element_type=F32)
    kr = kr_ref[...].astype(F32)
    k_rot = (kr * tk_c[...] + pltpu.roll(kr, LANE // 2, 1) * tk_s[...]).astype(CDT)
    for h in range(HEADS):
        lo = 2 * h * LANE
        q_rope = q[:, lo + LANE:lo + 2 * LANE]
        q_rot = q_rope * tq_c[...] + pltpu.roll(q_rope, LANE // 2, 1) * tq_s[...]
        q_ref[:, lo:lo + LANE] = q[:, lo:lo + LANE].astype(CDT)
        q_ref[:, lo + LANE:lo + 2 * LANE] = q_rot.astype(CDT)
        k_ref[:, lo:lo + LANE] = kv[:, lo:lo + LANE].astype(CDT)
        k_ref[:, lo + LANE:lo + 2 * LANE] = k_rot
        v_ref[:, h * LANE:(h + 1) * LANE] = kv[:, lo + LANE:lo + 2 * LANE].astype(CDT)


def _rope_tables(n, is_ctx):
    half = MLA_ROPE // 2
    nfreq = half // 2
    inv = ROPE_THETA ** (-np.arange(nfreq, dtype=np.float64) / nfreq)
    pos = np.arange(n)
    ang_r = (pos // GRID_W)[:, None] * inv
    ang_c = (pos % GRID_W)[:, None] * inv
    cos = np.concatenate([np.cos(ang_r), np.cos(ang_r), np.cos(ang_c), np.cos(ang_c)], axis=1)
    sin = np.concatenate([-np.sin(ang_r), np.sin(ang_r), -np.sin(ang_c), np.sin(ang_c)], axis=1)
    zero = np.zeros_like(cos)
    one = np.ones_like(cos)
    tq_c = np.concatenate([cos, zero], axis=1)
    tq_s = np.concatenate([sin, one], axis=1)
    if is_ctx:
        tk_c = np.concatenate([zero, zero], axis=1)
        tk_s = np.concatenate([zero, one], axis=1)
    else:
        tk_c = np.concatenate([cos, zero], axis=1)
        tk_s = np.concatenate([sin, zero], axis=1)
    return tuple(jnp.asarray(a, F32) for a in (tq_c, tq_s, tk_c, tk_s))


def _mla_proj(p1f, n, gq, gkv, wq, wkv, tables):
    t = p1f.shape[0]
    tm = _pick(n, 512)
    per_seq = n // tm
    wide = HEADS * 2 * LANE
    tab_spec = pl.BlockSpec((tm, LANE), lambda i: (i % per_seq, 0))
    return pl.pallas_call(
        functools.partial(_mla_proj_kernel, scale=(MLA_NOPE + MLA_ROPE) ** -0.5),
        grid=(t // tm,),
        in_specs=[pl.BlockSpec((tm, MLA_Q_LORA), lambda i: (i, BLK_C_Q * LANE // MLA_Q_LORA)),
                  pl.BlockSpec((tm, MLA_KV_LORA), lambda i: (i, BLK_C_KV * LANE // MLA_KV_LORA)),
                  pl.BlockSpec((tm, LANE), lambda i: (i, BLK_C_KR)),
                  pl.BlockSpec((1, MLA_Q_LORA), lambda i: (0, 0)),
                  pl.BlockSpec((1, MLA_KV_LORA), lambda i: (0, 0)),
                  pl.BlockSpec((MLA_Q_LORA, wide), lambda i: (0, 0)),
                  pl.BlockSpec((MLA_KV_LORA, wide), lambda i: (0, 0)),
                  tab_spec, tab_spec, tab_spec, tab_spec],
        out_specs=[pl.BlockSpec((tm, wide), lambda i: (i, 0)),
                   pl.BlockSpec((tm, wide), lambda i: (i, 0)),
                   pl.BlockSpec((tm, BR_W), lambda i: (i, 0))],
        out_shape=[jax.ShapeDtypeStruct((t, wide), CDT),
                   jax.ShapeDtypeStruct((t, wide), CDT),
                   jax.ShapeDtypeStruct((t, BR_W), CDT)],
        compiler_params=_cparams(("parallel",)),
        name="mla_projection",
    )(p1f, p1f, p1f, gq.reshape(1, -1), gkv.reshape(1, -1), wq, wkv, *tables)


def _mla_attn_kernel(q_ref, k_ref, v_ref, kc_ref, vc_ref, g_ref, o_ref, *, n, tk):
    q = q_ref[0]
    tq = q.shape[0]
    m = jnp.full((tq, 1), NEG, F32)
    l = jnp.zeros((tq, 1), F32)
    acc = jnp.zeros((tq, LANE), F32)

    def step(carry, k, v):
        m, l, acc = carry
        s = _qk(q, k)
        m_new = jnp.maximum(m, jnp.max(s, axis=-1, keepdims=True))
        alpha = jnp.exp(m - m_new)
        p = jnp.exp(s - m_new)
        l = alpha * l + jnp.sum(p, axis=-1, keepdims=True)
        acc = alpha * acc + jnp.dot(p.astype(CDT), v, preferred_element_type=F32)
        return m_new, l, acc

    carry = (m, l, acc)
    carry = step(carry, kc_ref[0], vc_ref[0])
    for c in range(n // tk):
        carry = step(carry, k_ref[0, pl.ds(c * tk, tk), :], v_ref[0, pl.ds(c * tk, tk), :])
    _, l, acc = carry
    o_ref[0] = (acc / l * _silu(g_ref[0].astype(F32))).astype(o_ref.dtype)


def _mla_attention(q, k, v, kc, vc, p1, bsz, n, nc):
    tq = _pick(n, 256)
    tk = _pick(n, 512)
    kd = 2 * LANE
    return pl.pallas_call(
        functools.partial(_mla_attn_kernel, n=n, tk=tk),
        grid=(bsz, HEADS, n // tq),
        in_specs=[pl.BlockSpec((1, tq, kd), lambda b, h, i: (b, i, h)),
                  pl.BlockSpec((1, n, kd), lambda b, h, i: (b, 0, h)),
                  pl.BlockSpec((1, n, LANE), lambda b, h, i: (b, 0, h)),
                  pl.BlockSpec((1, nc, kd), lambda b, h, i: (b, 0, h)),
                  pl.BlockSpec((1, nc, LANE), lambda b, h, i: (b, 0, h)),
                  pl.BlockSpec((1, tq, LANE), lambda b, h, i: (b, i, BLK_C_GATE + h))],
        out_specs=pl.BlockSpec((1, tq, LANE), lambda b, h, i: (b, i, h)),
        out_shape=jax.ShapeDtypeStruct((bsz, n, BR_W), CDT),
        compiler_params=_cparams(("parallel", "parallel", "arbitrary")),
        name="mla_attention",
    )(q, k, v, kc, vc, p1)


def _conv_kernel(x_ref, w_ref, o_ref, buf, *, n, k_scale):
    cb = pl.program_id(1)
    front = CONV_K // 2
    pad = 8
    buf[pl.ds(0, pad), :] = jnp.zeros((pad, LANE), F32)
    buf[pl.ds(pad + n, pad), :] = jnp.zeros((pad, LANE), F32)
    buf[pl.ds(pad, n), :] = x_ref[0].astype(F32)
    acc = None
    for j in range(CONV_K):
        term = buf[pl.ds(pad + j - front, n), :] * w_ref[j:j + 1, :]
        acc = term if acc is None else acc + term
    y = _silu(acc) * jnp.where(cb >= HEADS, k_scale, 1.0)
    o_ref[0] = y.astype(o_ref.dtype)


def _short_conv(p1, conv_w):
    bsz, n, _ = p1.shape
    ncb = 2 * HEADS
    return pl.pallas_call(
        functools.partial(_conv_kernel, n=n, k_scale=HEAD_DIM ** -0.5),
        grid=(bsz, ncb),
        in_specs=[pl.BlockSpec((1, n, LANE), lambda b, c: (b, 0, BLK_D_Q + c)),
                  pl.BlockSpec((CONV_K, LANE), lambda b, c: (0, c))],
        out_specs=pl.BlockSpec((1, n, LANE), lambda b, c: (b, 0, c)),
        out_shape=jax.ShapeDtypeStruct((bsz, n, 2 * BR_W), CDT),
        scratch_shapes=[pltpu.VMEM((n + 16, LANE), F32)],
        compiler_params=_cparams(("parallel", "arbitrary")),
        name="mlstm_conv",
    )(p1, conv_w)


def _scan_rows(x, op, fill, reverse):
    n = x.shape[0]
    t = lax.broadcasted_iota(jnp.int32, x.shape, 0)
    k = 1
    while k < n:
        if reverse:
            sh = jnp.where(t < n - k, pltpu.roll(x, n - k, 0), fill)
        else:
            sh = jnp.where(t >= k, pltpu.roll(x, k, 0), fill)
        x = op(x, sh)
        k *= 2
    return x


def _mlstm_kernel(qkf_ref, vf_ref, gf_ref, qkb_ref, vb_ref, gb_ref, bif_ref,
                  hf_ref, hb_ref, c_ref, m_ref):
    j = pl.program_id(1)
    L = ML_CHUNK

    @pl.when(j == 0)
    def _():
        c_ref[...] = jnp.zeros(c_ref.shape, F32)
        m_ref[...] = jnp.zeros(m_ref.shape, F32)

    row = lax.broadcasted_iota(jnp.int32, (L, L), 0)
    col = lax.broadcasted_iota(jnp.int32, (L, L), 1)
    ones_col = jnp.where(lax.broadcasted_iota(jnp.int32, (L, LANE), 1) == 0, 1.0, 0.0).astype(CDT)

    for d, (qk_ref, v_ref, g_ref, h_ref) in enumerate(((qkf_ref, vf_ref, gf_ref, hf_ref),
                                                         (qkb_ref, vb_ref, gb_ref, hb_ref))):
        rev = d == 1
        g = g_ref[0] + bif_ref[...]
        li = g[:, :LANE]
        lf = jax.nn.log_sigmoid(g[:, LANE:])
        bc = _scan_rows(lf, jnp.add, 0.0, rev)
        r = li - bc
        cm = _scan_rows(r, jnp.maximum, NEG, rev)
        m_row = m_ref[d, 0:1, :]
        big_m = jnp.maximum(m_row, cm)
        a_all = jnp.exp(m_row - big_m)
        floor_all = jnp.exp(-(bc + big_m))
        m_fin = jnp.maximum(m_row, jnp.max(r, axis=0, keepdims=True))
        wk_all = jnp.exp(r - m_fin)
        decay_all = jnp.exp(m_row - m_fin)
        b_last = bc[0:1, :] if rev else bc[L - 1:L, :]
        m_ref[d, 0:1, :] = b_last + m_fin
        r_t = r.T
        causal = (col >= row) if rev else (col <= row)

        for h in range(HEADS):
            c = d * HEADS + h
            q = qk_ref[0, :, h * LANE:(h + 1) * LANE]
            k = qk_ref[0, :, (HEADS + h) * LANE:(HEADS + h + 1) * LANE]
            v_ext = jnp.concatenate([v_ref[0, :, h * LANE:(h + 1) * LANE], ones_col], axis=1)
            dm = jnp.exp(jnp.where(causal, r_t[c:c + 1, :] - big_m[:, c:c + 1], NEG))
            s = (_qk(q, k) * dm).astype(CDT)
            c_old = c_ref[c]
            nd = (a_all[:, c:c + 1] * jnp.dot(q, c_old.astype(CDT), preferred_element_type=F32)
                  + jnp.dot(s, v_ext, preferred_element_type=F32))
            den = jnp.maximum(jnp.abs(nd[:, LANE:LANE + 1]), floor_all[:, c:c + 1])
            h_ref[0, :, h * LANE:(h + 1) * LANE] = nd[:, :LANE] / den
            kw_t = (k.astype(F32) * wk_all[:, c:c + 1]).T.astype(CDT)
            upd = jnp.dot(kw_t, v_ext, preferred_element_type=F32)
            c_ref[c] = decay_all[:, c:c + 1] * c_old + upd


def _mlstm_scan(qk_cat, v_cat, g_cat, b_if2, bsz, n, nc):
    L = ML_CHUNK
    steps = (n + nc) // L
    return pl.pallas_call(
        _mlstm_kernel,
        grid=(bsz, steps),
        in_specs=[pl.BlockSpec((1, L, 2 * BR_W), lambda b, j: (b, j, 0)),
                  pl.BlockSpec((1, L, BR_W), lambda b, j: (b, j, 0)),
                  pl.BlockSpec((1, L, 2 * LANE), lambda b, j: (b, j, 0)),
                  pl.BlockSpec((1, L, 2 * BR_W), lambda b, j: (b, steps - j, 0)),
                  pl.BlockSpec((1, L, BR_W), lambda b, j: (b, steps - j, 0)),
                  pl.BlockSpec((1, L, 2 * LANE), lambda b, j: (b, steps - j, 0)),
                  pl.BlockSpec((1, 2 * LANE), lambda b, j: (0, 0))],
        out_specs=[pl.BlockSpec((1, L, BR_W), lambda b, j: (b, j, 0)),
                   pl.BlockSpec((1, L, BR_W), lambda b, j: (b, steps - 1 - j, 0))],
        out_shape=[jax.ShapeDtypeStruct((bsz, n + nc, BR_W), F32),
                   jax.ShapeDtypeStruct((bsz, n + nc, BR_W), F32)],
        scratch_shapes=[pltpu.VMEM((2 * HEADS, LANE, 2 * LANE), F32),
                        pltpu.VMEM((2, 8, LANE), F32)],
        compiler_params=_cparams(("parallel", "arbitrary")),
        name="mlstm_scan",
    )(qk_cat, v_cat, g_cat, qk_cat, v_cat, g_cat, b_if2)


def _mlstm_out_kernel(hf_ref, hb_ref, o_ref, g_ref, gn_ref, y_ref):
    for h in range(HEADS):
        sl = slice(h * LANE, (h + 1) * LANE)
        hh = hf_ref[0, :, sl] + hb_ref[0, :, sl]
        hn = hh * lax.rsqrt(jnp.mean(hh * hh, axis=-1, keepdims=True) + EPS)
        y = hn * gn_ref[:, sl] * jax.nn.sigmoid(o_ref[0, :, sl].astype(F32))
        y_ref[0, :, sl] = (y * _silu(g_ref[0, :, sl].astype(F32))).astype(y_ref.dtype)


def _mlstm_output(h_f, h_b, p1, gnorm, f_off, b_off):
    bsz, n, _ = p1.shape
    L = ML_CHUNK
    wblk = BR_W // LANE
    return pl.pallas_call(
        _mlstm_out_kernel,
        grid=(bsz, n // L),
        in_specs=[pl.BlockSpec((1, L, BR_W), lambda b, i: (b, f_off + i, 0)),
                  pl.BlockSpec((1, L, BR_W), lambda b, i: (b, b_off + i, 0)),
                  pl.BlockSpec((1, L, BR_W), lambda b, i: (b, i, BLK_D_O // wblk)),
                  pl.BlockSpec((1, L, BR_W), lambda b, i: (b, i, BLK_D_GATE // wblk)),
                  pl.BlockSpec((1, BR_W), lambda b, i: (0, 0))],
        out_specs=pl.BlockSpec((1, L, BR_W), lambda b, i: (b, i, 0)),
        out_shape=jax.ShapeDtypeStruct((bsz, n, BR_W), CDT),
        compiler_params=_cparams(("parallel", "parallel")),
        name="mlstm_output",
    )(h_f, h_b, p1, p1, gnorm.reshape(1, BR_W))


def _merge_kernel(x_ref, mod_ref, ya, yb, yc, yd, mp_ref, wbr_ref, wout_ref, gfin_ref, o_ref, *, d, final_norm):
    acc = None
    for i, y_ref in enumerate((ya, yb, yc, yd)):
        z = jnp.dot(y_ref[...], wbr_ref[i], preferred_element_type=F32)
        t = jax.nn.sigmoid(mp_ref[:, i * d:(i + 1) * d].astype(F32)) * z
        acc = t if acc is None else acc + t
    out = jnp.dot(acc.astype(CDT), wout_ref[...], preferred_element_type=F32)
    x = x_ref[...] + mod_ref[0, 2:3, :] * out
    if final_norm:
        x = x * lax.rsqrt(jnp.mean(x * x, axis=-1, keepdims=True) + EPS) * gfin_ref[...]
    o_ref[...] = x


def _merge(x2, mod, mod_row, ys, mp, w_br, w_out, g_final, final_norm, tm):
    t, d = x2.shape
    y_spec = pl.BlockSpec((tm, BR_W), lambda i: (i, 0))
    return pl.pallas_call(
        functools.partial(_merge_kernel, d=d, final_norm=final_norm),
        grid=(t // tm,),
        in_specs=[pl.BlockSpec((tm, d), lambda i: (i, 0)),
                  pl.BlockSpec((1, 3, d), lambda i: (mod_row(i), 0, 0)),
                  y_spec, y_spec, y_spec, y_spec,
                  pl.BlockSpec((tm, N_BRANCH * d), lambda i: (i, 0)),
                  pl.BlockSpec((N_BRANCH, BR_W, d), lambda i: (0, 0, 0), pipeline_mode=pl.Buffered(1)),
                  pl.BlockSpec((d, d), lambda i: (0, 0), pipeline_mode=pl.Buffered(1)),
                  pl.BlockSpec((1, d), lambda i: (0, 0))],
        out_specs=pl.BlockSpec((tm, d), lambda i: (i, 0)),
        out_shape=jax.ShapeDtypeStruct((t, d), F32),
        compiler_params=_cparams(("parallel",)),
        name="merge",
    )(x2, mod, *ys, mp, w_br, w_out, g_final.reshape(1, d))


def _split_offsets(d):
    widths = (3 * BR_W, BR_W, BR_W, BR_W, MLA_Q_LORA, MLA_KV_LORA, MLA_ROPE, BR_W,
              3 * BR_W, BR_W, 4 * HEADS, BR_W, N_BRANCH * d)
    names = ('a_qkv', 'a_gate', 'b_in', 'b_gate', 'c_q', 'c_kv', 'c_kr', 'c_gate',
             'd_qkv', 'd_o', 'd_if', 'd_gate', 'merge')
    offs = np.concatenate([[0], np.cumsum(widths)])
    return {nm: (int(offs[i]), int(offs[i + 1])) for i, nm in enumerate(names)}


def _rope_swap_perm():
    half = MLA_ROPE // 2
    q = half // 2
    return np.concatenate([np.arange(q, half), np.arange(0, q), np.arange(half + q, MLA_ROPE), np.arange(half, half + q)])


def _prep_w_in(w_in):
    nl, d, _ = w_in.shape
    o = _split_offsets(d)
    cut = lambda nm: w_in[:, :, o[nm][0]:o[nm][1]]
    kr = cut('c_kr')
    d_if = cut('d_if')
    d_if = d_if.reshape(nl, d, 2, 2, HEADS)
    zpad = jnp.zeros((nl, d, LANE - 2 * HEADS), w_in.dtype)
    li = jnp.concatenate([d_if[:, :, :, 0, :].reshape(nl, d, 2 * HEADS), zpad], axis=-1)
    lf = jnp.concatenate([d_if[:, :, :, 1, :].reshape(nl, d, 2 * HEADS), zpad], axis=-1)
    w1 = jnp.concatenate([
        w_in[:, :, :o['c_kv'][1]], cut('c_gate'), cut('d_qkv'), cut('d_o'), cut('d_gate'),
        kr, kr[:, :, _rope_swap_perm()], jnp.zeros((nl, d, (BLK_D_LI - BLK_C_KR - 1) * LANE), w_in.dtype),
        li, lf], axis=-1)
    assert w1.shape[-1] == N1, w1.shape
    return w1.astype(CDT), cut('merge').astype(CDT)


def _prep_b_if(b_if):
    nl = b_if.shape[0]
    b = b_if.reshape(nl, 2, 2, HEADS)
    zpad = jnp.zeros((nl, LANE - 2 * HEADS), b_if.dtype)
    li = jnp.concatenate([b[:, :, 0, :].reshape(nl, 2 * HEADS), zpad], axis=-1)
    lf = jnp.concatenate([b[:, :, 1, :].reshape(nl, 2 * HEADS), zpad], axis=-1)
    return jnp.concatenate([li, lf], axis=-1).reshape(nl, 1, 2 * LANE)


def _prep_w_uq(w_uq):
    nl, r, _ = w_uq.shape
    w = w_uq.reshape(nl, r, HEADS, MLA_NOPE + MLA_ROPE)
    rope = w[..., MLA_NOPE:]
    w = jnp.concatenate([w[..., :MLA_NOPE], rope, rope[..., _rope_swap_perm()]], axis=-1)
    return w.reshape(nl, r, HEADS * 2 * LANE).astype(CDT)


def kernel(x, c, ctx, c_ctx, norm_g, w_mod, b_mod, w_in, na_rpb, pool_w, pool_scale, mla_gq, mla_gkv, w_uq, w_ukv,
           conv_w, b_if, ml_gnorm, w_br, w_out, g_final):
    bsz, n, d = x.shape
    nc = ctx.shape[1]
    depth = w_in.shape[0]
    rows = n // GRID_W
    assert n % (NA_QROWS * GRID_W) == 0 and rows >= NA_KROWS
    assert nc == ML_CHUNK and n % ML_CHUNK == 0 and nc % 8 == 0

    w1_all, w2_all = _prep_w_in(w_in)
    wq_all = _prep_w_uq(w_uq)
    wkv_all = w_ukv.astype(CDT)
    wbr_all = w_br.astype(CDT)
    wout_all = w_out.astype(CDT)
    poolw_all = pool_w.astype(CDT)
    bif_all = _prep_b_if(b_if)
    bias_all = _na_bias_tables(na_rpb, rows)
    tab_lat = _rope_tables(n, False)
    tab_ctx = _rope_tables(nc, True)

    n_cond = -(-(bsz + 1) // 8) * 8
    cond = jnp.concatenate([c, c_ctx[None, :], jnp.zeros((n_cond - bsz - 1, d), F32)], axis=0)
    mods = _modulation(cond, w_mod, b_mod)
    mods = mods.reshape(depth, n_cond, 3, d)

    tm_l = _pick(n, 1024)
    tm_c = _pick(bsz * nc, 1024)
    tn1 = _pick(N1, 1536) if N1 % 1536 == 0 else _pick(N1, 512)
    tn2 = _pick(N_BRANCH * d, 1024)
    tm_merge = _pick(n, 256)
    lat_row = lambda i: i // (n // tm_l)
    ctx_row = lambda i: bsz
    lat_row_m = lambda i: i // (n // tm_merge)

    xl = x.reshape(bsz * n, d)
    xc = ctx.reshape(bsz * nc, d)
    for l in range(depth):
        need_ctx = l < depth - 1
        last = l == depth - 1
        mod = mods[l]
        p1, gates = _inproj(xl, mod, lat_row, norm_g[l], w1_all[l], tm_l, tn1, 2 * LANE)
        p2 = _inproj(xl, mod, lat_row, norm_g[l], w2_all[l], tm_l, tn2, 0)[0]
        p1c, gates_c = _inproj(xc, mod, ctx_row, norm_g[l], w1_all[l], tm_c, tn1, 2 * LANE)
        p1_3 = p1.reshape(bsz, n, N1)
        p1c_3 = p1c.reshape(bsz, nc, N1)

        ya = _neighbourhood(p1_3, p1c_3, bias_all[l], bsz, n, nc)
        yb = _pool(p1_3, poolw_all[l], pool_scale[l])
        q_l, k_l, v_l = _mla_proj(p1, n, mla_gq[l], mla_gkv[l], wq_all[l], wkv_all[l], tab_lat)
        q_c, k_c, v_c = _mla_proj(p1c, nc, mla_gq[l], mla_gkv[l], wq_all[l], wkv_all[l], tab_ctx)
        r3 = lambda a, m: a.reshape(bsz, m, a.shape[-1])
        yc = _mla_attention(r3(q_l, n), r3(k_l, n), r3(v_l, n), r3(k_c, nc), r3(v_c, nc), p1_3, bsz, n, nc)
        qk_l = _short_conv(p1_3, conv_w[l])
        qk_c = _short_conv(p1c_3, conv_w[l])
        vsl = slice(BLK_D_V * LANE, (BLK_D_V + HEADS) * LANE)
        qk_cat = jnp.concatenate([qk_c, qk_l, qk_c], axis=1)
        v_cat = jnp.concatenate([p1c_3[:, :, vsl], p1_3[:, :, vsl], p1c_3[:, :, vsl]], axis=1)
        g_l = gates.reshape(bsz, n, 2 * LANE)
        g_c = gates_c.reshape(bsz, nc, 2 * LANE)
        g_cat = jnp.concatenate([g_c, g_l, g_c], axis=1)
        h_f, h_b = _mlstm_scan(qk_cat, v_cat, g_cat, bif_all[l], bsz, n, nc)
        yd = _mlstm_output(h_f, h_b, p1_3, ml_gnorm[l], nc // ML_CHUNK, 0)

        f2 = lambda a: a.reshape(-1, a.shape[-1])
        xl_new = _merge(xl, mod, lat_row_m, (f2(ya), f2(yb), f2(yc), f2(yd)), p2, wbr_all[l], wout_all[l],
                        g_final, last, tm_merge)
        if need_ctx:
            p2c = _inproj(xc, mod, ctx_row, norm_g[l], w2_all[l], tm_c, tn2, 0)[0]
            ya_c = _ctx_attention(p1c_3, BLK_A_Q, LANE, p1c_3, BLK_A_K, LANE, p1c_3, BLK_A_V, p1c_3, BLK_A_GATE,
                                  HEAD_DIM ** -0.5)
            yb_c = _pool(p1c_3, poolw_all[l], pool_scale[l])
            yc_c = _ctx_attention(r3(q_c, nc), 0, 2 * LANE, r3(k_c, nc), 0, 2 * LANE, r3(v_c, nc), 0,
                                  p1c_3, BLK_C_GATE, 1.0)
            yd_c = _mlstm_output(h_f, h_b, p1c_3, ml_gnorm[l], 0, n // ML_CHUNK)
            xc = _merge(xc, mod, ctx_row, (f2(ya_c), f2(yb_c), f2(yc_c), f2(yd_c)), p2c, wbr_all[l], wout_all[l],
                        g_final, False, _pick(nc * bsz, 256))
        xl = xl_new
    return xl.reshape(bsz, n, d)
```

```python
import functools

import numpy as np
import jax
import jax.numpy as jnp
from jax import lax
from jax.experimental import pallas as pl
from jax.experimental.pallas import tpu as pltpu

F32 = jnp.float32
CDT = jnp.bfloat16

GRID_W = 64
EPS = 1e-6
N_BRANCH = 4
BR_W = 512
HEADS = 4
HEAD_DIM = BR_W // HEADS
NA_ROWS = 8
NA_COLS = 16
POOL_WINDOWS = (2, 4, 8, 16)
MLA_NOPE = 128
MLA_ROPE = 64
MLA_Q_LORA = 512
MLA_KV_LORA = 512
ROPE_THETA = 10000.0
CONV_K = 4

LANE = 128
VMEM_LIMIT = 56 * 1024 * 1024
NEG = -1e30

BLK_A_Q, BLK_A_K, BLK_A_V, BLK_A_GATE = 0, 4, 8, 12
BLK_B_IN, BLK_B_GATE = 16, 20
BLK_C_Q, BLK_C_KV, BLK_C_GATE = 24, 28, 32
BLK_D_Q, BLK_D_K, BLK_D_V, BLK_D_O, BLK_D_GATE = 36, 40, 44, 48, 52
BLK_C_KR = 56
BLK_D_LI, BLK_D_LF = 58, 59
N1_BLOCKS = 60
N1 = N1_BLOCKS * LANE

NA_QROWS = 8
NA_KROWS = 16
ML_CHUNK = 256


def _cparams(sem):
    return pltpu.CompilerParams(dimension_semantics=sem, vmem_limit_bytes=VMEM_LIMIT)


def _silu(x):
    return x * jax.nn.sigmoid(x)


def _pick(n, pref):
    t = min(n, pref)
    while n % t:
        t //= 2
    return t


def _mod_kernel(c_ref, w_ref, b_ref, o_ref):
    c = _silu(c_ref[...]).astype(CDT)
    o_ref[0] = jnp.dot(c, w_ref[0].astype(CDT), preferred_element_type=F32) + b_ref[0]


def _modulation(cond, w_mod, b_mod):
    nl, d, d3 = w_mod.shape
    r = cond.shape[0]
    tn = _pick(d3, 1024)
    return pl.pallas_call(
        _mod_kernel,
        grid=(nl, d3 // tn),
        in_specs=[pl.BlockSpec((r, d), lambda l, j: (0, 0)),
                  pl.BlockSpec((1, d, tn), lambda l, j: (l, 0, j)),
                  pl.BlockSpec((1, 1, tn), lambda l, j: (l, 0, j))],
        out_specs=pl.BlockSpec((1, r, tn), lambda l, j: (l, 0, j)),
        out_shape=jax.ShapeDtypeStruct((nl, r, d3), F32),
        compiler_params=_cparams(("parallel", "parallel")),
        name="modulation",
    )(cond, w_mod, b_mod.reshape(nl, 1, d3))


def _inproj_kernel(x_ref, mod_ref, g_ref, w_ref, o_ref, *rest, n_f32_cols):
    hn_ref = rest[-1]
    j = pl.program_id(1)

    @pl.when(j == 0)
    def _():
        x = x_ref[...]
        y = x * lax.rsqrt(jnp.mean(x * x, axis=-1, keepdims=True) + EPS) * g_ref[...]
        hn = y * (1.0 + mod_ref[0, 1:2, :]) + mod_ref[0, 0:1, :]
        hn_ref[...] = hn.astype(CDT)

    acc = jnp.dot(hn_ref[...], w_ref[...], preferred_element_type=F32)
    o_ref[...] = acc.astype(o_ref.dtype)
    if n_f32_cols:
        f_ref = rest[0]

        @pl.when(j == pl.num_programs(1) - 1)
        def _():
            f_ref[...] = acc[:, acc.shape[1] - n_f32_cols:]


def _inproj(x2, mod, mod_row, norm_g, w, tm, tn, n_f32_cols):
    t, d = x2.shape
    n = w.shape[1]
    out_shape = [jax.ShapeDtypeStruct((t, n), CDT)]
    out_specs = [pl.BlockSpec((tm, tn), lambda i, j: (i, j))]
    if n_f32_cols:
        out_shape.append(jax.ShapeDtypeStruct((t, n_f32_cols), F32))
        out_specs.append(pl.BlockSpec((tm, n_f32_cols), lambda i, j: (i, 0)))
    res = pl.pallas_call(
        functools.partial(_inproj_kernel, n_f32_cols=n_f32_cols),
        grid=(t // tm, n // tn),
        in_specs=[pl.BlockSpec((tm, d), lambda i, j: (i, 0)),
                  pl.BlockSpec((1, 3, d), lambda i, j: (mod_row(i), 0, 0)),
                  pl.BlockSpec((1, d), lambda i, j: (0, 0)),
                  pl.BlockSpec((d, tn), lambda i, j: (0, j))],
        out_specs=out_specs,
        out_shape=out_shape,
        scratch_shapes=[pltpu.VMEM((tm, d), CDT)],
        compiler_params=_cparams(("parallel", "arbitrary")),
        name="inproj",
    )(x2, mod, norm_g.reshape(1, d), w)
    return res


def _softmax_pv(parts, exp=jnp.exp):
    m = None
    for s, _ in parts:
        mi = jnp.max(s, axis=-1, keepdims=True)
        m = mi if m is None else jnp.maximum(m, mi)
    l = None
    o = None
    for s, v in parts:
        p = exp(s - m)
        li = jnp.sum(p, axis=-1, keepdims=True)
        oi = jnp.dot(p.astype(CDT), v, preferred_element_type=F32)
        l = li if l is None else l + li
        o = oi if o is None else o + oi
    return o / l


def _qk(q, k):
    return lax.dot_general(q, k, (((1,), (1,)), ((), ())), preferred_element_type=F32)


def _na_row_offsets(rows, jt):
    start = int(np.clip(jt * NA_QROWS - NA_ROWS // 2, 0, rows - NA_KROWS))
    r = jt * NA_QROWS + np.arange(NA_QROWS)
    keyrow = start + np.arange(NA_KROWS)
    rs = np.clip(r - NA_ROWS // 2, 0, rows - NA_ROWS)
    row_ok = (keyrow[None, :] >= rs[:, None]) & (keyrow[None, :] < rs[:, None] + NA_ROWS)
    return np.where(row_ok, keyrow[None, :] - r[:, None] + NA_ROWS - 1, -1)


def _na_kernel(q_ref, k_ref, v_ref, kc_ref, vc_ref, g_ref, bc_ref, o_ref, bias_ref, *, rows, scale):
    j = pl.program_id(2)
    nj = rows // NA_QROWS
    w = GRID_W
    left = lax.broadcasted_iota(jnp.int32, (w, LANE), 1) < w

    def build(jt):
        roff = _na_row_offsets(rows, jt)
        neg = jnp.full((w, LANE), NEG, F32)
        for g in range(NA_QROWS):
            for kp in range(NA_KROWS // 2):
                ra, rb = int(roff[g, 2 * kp]), int(roff[g, 2 * kp + 1])
                ta = bc_ref[0, ra] if ra >= 0 else neg
                tb = bc_ref[0, rb] if rb >= 0 else neg
                tile = neg if (ra < 0 and rb < 0) else jnp.where(left, ta, tb)
                bias_ref[g * w:(g + 1) * w, kp * LANE:(kp + 1) * LANE] = tile

    pl.when(j == 0)(lambda: build(0))
    if nj > 2:
        pl.when(j == 1)(lambda: build(1))
    pl.when(j == nj - 1)(lambda: build(nj - 1))

    start = jnp.clip(j * NA_QROWS - NA_ROWS // 2, 0, rows - NA_KROWS) * GRID_W
    start = pl.multiple_of(start, GRID_W)
    kw = k_ref[0, pl.ds(start, NA_KROWS * GRID_W), :]
    vw = v_ref[0, pl.ds(start, NA_KROWS * GRID_W), :]
    q = q_ref[0]
    s_loc = _qk(q, kw) * scale + bias_ref[...]
    s_ctx = _qk(q, kc_ref[0]) * scale
    o = _softmax_pv([(s_loc, vw), (s_ctx, vc_ref[0])])
    o_ref[0] = (o * _silu(g_ref[0].astype(F32))).astype(o_ref.dtype)


def _na_column_tables(rpb):
    w = GRID_W
    nl, nh, n_ro, n_co = rpb.shape
    cq = np.arange(w)
    cstart = np.clip(cq - NA_COLS // 2, 0, w - NA_COLS)
    col_ok = (cq[None, :] >= cstart[:, None]) & (cq[None, :] < cstart[:, None] + NA_COLS)
    period = 2 * w + 1
    v = jnp.concatenate([rpb[..., NA_COLS - 1:], jnp.zeros((nl, nh, n_ro, period - n_co), rpb.dtype),
                         rpb[..., :NA_COLS - 1]], axis=-1).astype(F32)
    t = jnp.tile(v, (1, 1, 1, w))[..., :w * 2 * w].reshape(nl, nh, n_ro, w, 2 * w)[..., :w]
    t = jnp.where(jnp.asarray(col_ok), t, NEG)
    return jnp.concatenate([t, t], axis=-1)


def _neighbourhood(p1, p1c, bc, bsz, n, nc):
    rows = n // GRID_W
    nj = rows // NA_QROWS
    tq = NA_QROWS * GRID_W
    n_ro = bc.shape[1]
    return pl.pallas_call(
        functools.partial(_na_kernel, rows=rows, scale=HEAD_DIM ** -0.5),
        grid=(bsz, HEADS, nj),
        in_specs=[pl.BlockSpec((1, tq, LANE), lambda b, h, j: (b, j, BLK_A_Q + h)),
                  pl.BlockSpec((1, n, LANE), lambda b, h, j: (b, 0, BLK_A_K + h)),
                  pl.BlockSpec((1, n, LANE), lambda b, h, j: (b, 0, BLK_A_V + h)),
                  pl.BlockSpec((1, nc, LANE), lambda b, h, j: (b, 0, BLK_A_K + h)),
                  pl.BlockSpec((1, nc, LANE), lambda b, h, j: (b, 0, BLK_A_V + h)),
                  pl.BlockSpec((1, tq, LANE), lambda b, h, j: (b, j, BLK_A_GATE + h)),
                  pl.BlockSpec((1, n_ro, GRID_W, 2 * GRID_W), lambda b, h, j: (h, 0, 0, 0))],
        out_specs=pl.BlockSpec((1, tq, LANE), lambda b, h, j: (b, j, h)),
        out_shape=jax.ShapeDtypeStruct((bsz, n, BR_W), CDT),
        scratch_shapes=[pltpu.VMEM((tq, NA_KROWS * GRID_W), F32)],
        compiler_params=_cparams(("parallel", "parallel", "arbitrary")),
        name="neighbourhood_attention",
    )(p1, p1, p1, p1c, p1c, p1, bc)


def _ctx_attn_kernel(q_ref, k_ref, v_ref, g_ref, o_ref, *, scale, base2):
    s = _qk(q_ref[0], k_ref[0]) * scale
    o = _softmax_pv([(s, v_ref[0])], jnp.exp2 if base2 else jnp.exp)
    o_ref[0] = (o * _silu(g_ref[0].astype(F32))).astype(o_ref.dtype)


def _ctx_attention(q, qblk, qw, k, kblk, kw, v, vblk, vstep, gate, gblk, scale, base2):
    bsz, nc, _ = q.shape
    return pl.pallas_call(
        functools.partial(_ctx_attn_kernel, scale=scale, base2=base2),
        grid=(bsz, HEADS),
        in_specs=[pl.BlockSpec((1, nc, qw), lambda b, h: (b, 0, qblk + h)),
                  pl.BlockSpec((1, nc, kw), lambda b, h: (b, 0, kblk + h)),
                  pl.BlockSpec((1, nc, LANE), lambda b, h: (b, 0, vblk + vstep * h)),
                  pl.BlockSpec((1, nc, LANE), lambda b, h: (b, 0, gblk + h))],
        out_specs=pl.BlockSpec((1, nc, LANE), lambda b, h: (b, 0, h)),
        out_shape=jax.ShapeDtypeStruct((bsz, nc, BR_W), CDT),
        compiler_params=_cparams(("parallel", "parallel")),
        name="context_attention",
    )(q, k, v, gate)


POOL_PAD = 16
POOL_HALO = 8


def _pool_kernel(u_ref, g_ref, w_ref, sc_ref, o_ref, buf_a, buf_b, *, n):
    gi = pl.program_id(1)
    x = u_ref[0].astype(F32)
    t = lax.broadcasted_iota(jnp.int32, (n, LANE), 0)
    ext = n + 2 * POOL_HALO

    def finish(wsum, w):
        lo = jnp.maximum(t - w // 2, 0)
        hi = jnp.minimum(t + (w - 1 - w // 2), n - 1)
        cnt = (hi - lo + 1).astype(F32)
        d = wsum / cnt - x
        y = jnp.dot(d.astype(CDT), w_ref[0], preferred_element_type=F32) * sc_ref[...]
        o_ref[0] = (y * _silu(g_ref[0].astype(F32))).astype(o_ref.dtype)

    def window(ref, off):
        return ref[pl.ds(POOL_PAD + off, n), :]

    def wide(ref, off):
        return ref[pl.ds(POOL_PAD - POOL_HALO + off, ext), :]

    for buf in (buf_a, buf_b):
        buf[pl.ds(0, POOL_PAD), :] = jnp.zeros((POOL_PAD, LANE), F32)
        buf[pl.ds(POOL_PAD + n, POOL_PAD), :] = jnp.zeros((POOL_PAD, LANE), F32)
    buf_a[pl.ds(POOL_PAD, n), :] = x

    for level, w in enumerate(POOL_WINDOWS):
        @pl.when(gi == level)
        def _(level=level, w=w):
            if level == 0:
                finish(window(buf_a, -1) + window(buf_a, 0), w)
                return
            buf_b[pl.ds(POOL_PAD - POOL_HALO, ext), :] = wide(buf_a, -1) + wide(buf_a, 0)
            src, dst = buf_b, buf_a
            for lv in range(1, level):
                h = 1 << (lv - 1)
                dst[pl.ds(POOL_PAD - POOL_HALO, ext), :] = wide(src, -h) + wide(src, h)
                src, dst = dst, src
            h = 1 << (level - 1)
            finish(window(src, -h) + window(src, h), w)


def _pool(p1, pool_w, pool_scale):
    bsz, n, _ = p1.shape
    ng = len(POOL_WINDOWS)
    return pl.pallas_call(
        functools.partial(_pool_kernel, n=n),
        grid=(bsz, ng),
        in_specs=[pl.BlockSpec((1, n, LANE), lambda b, g: (b, 0, BLK_B_IN + g)),
                  pl.BlockSpec((1, n, LANE), lambda b, g: (b, 0, BLK_B_GATE + g)),
                  pl.BlockSpec((1, LANE, LANE), lambda b, g: (g, 0, 0)),
                  pl.BlockSpec((1, LANE), lambda b, g: (0, g))],
        out_specs=pl.BlockSpec((1, n, LANE), lambda b, g: (b, 0, g)),
        out_shape=jax.ShapeDtypeStruct((bsz, n, BR_W), CDT),
        scratch_shapes=[pltpu.VMEM((n + 2 * POOL_PAD, LANE), F32),
                        pltpu.VMEM((n + 2 * POOL_PAD, LANE), F32)],
        compiler_params=_cparams(("parallel", "arbitrary")),
        name="pool_branch",
    )(p1, p1, pool_w, pool_scale.reshape(1, BR_W))


def _rms(x, g):
    return x * lax.rsqrt(jnp.mean(x * x, axis=-1, keepdims=True) + EPS) * g


def _mla_proj_kernel(cq_ref, ckv_ref, kr_ref, gq_ref, gkv_ref, wq_ref, wkv_ref,
                     tq_c, tq_s, tk_c, tk_s, q_ref, k_ref, v_ref, *, scale):
    hq = _rms(cq_ref[...].astype(F32), gq_ref[...]).astype(CDT)
    hkv = _rms(ckv_ref[...].astype(F32), gkv_ref[...]).astype(CDT)
    q = jnp.dot(hq, wq_ref[...], preferred_element_type=F32) * scale
    kv = jnp.dot(hkv, wkv_ref[...], preferred_element_type=F32)
    kr = kr_ref[...].astype(F32)
    k_rot = (kr * tk_c[...] + pltpu.roll(kr, LANE // 2, 1) * tk_s[...]).astype(CDT)
    ones_col = jnp.where(lax.broadcasted_iota(jnp.int32, kr.shape, 1) == 0, 1.0, 0.0).astype(CDT)
    for h in range(HEADS):
        lo = 2 * h * LANE
        q_rope = q[:, lo + LANE:lo + 2 * LANE]
        q_rot = q_rope * tq_c[...] + pltpu.roll(q_rope, LANE // 2, 1) * tq_s[...]
        q_ref[:, lo:lo + LANE] = q[:, lo:lo + LANE].astype(CDT)
        q_ref[:, lo + LANE:lo + 2 * LANE] = q_rot.astype(CDT)
        k_ref[:, lo:lo + LANE] = kv[:, lo:lo + LANE].astype(CDT)
        k_ref[:, lo + LANE:lo + 2 * LANE] = k_rot
        v_ref[:, lo:lo + LANE] = kv[:, lo + LANE:lo + 2 * LANE].astype(CDT)
        v_ref[:, lo + LANE:lo + 2 * LANE] = ones_col


def _rope_tables(n, is_ctx):
    half = MLA_ROPE // 2
    nfreq = half // 2
    inv = ROPE_THETA ** (-np.arange(nfreq, dtype=np.float64) / nfreq)
    pos = np.arange(n)
    ang_r = (pos // GRID_W)[:, None] * inv
    ang_c = (pos % GRID_W)[:, None] * inv
    cos = np.concatenate([np.cos(ang_r), np.cos(ang_r), np.cos(ang_c), np.cos(ang_c)], axis=1)
    sin = np.concatenate([-np.sin(ang_r), np.sin(ang_r), -np.sin(ang_c), np.sin(ang_c)], axis=1)
    zero = np.zeros_like(cos)
    one = np.ones_like(cos)
    tq_c = np.concatenate([cos, zero], axis=1)
    tq_s = np.concatenate([sin, one], axis=1)
    if is_ctx:
        tk_c = np.concatenate([zero, zero], axis=1)
        tk_s = np.concatenate([zero, one], axis=1)
    else:
        tk_c = np.concatenate([cos, zero], axis=1)
        tk_s = np.concatenate([sin, zero], axis=1)
    return tuple(jnp.asarray(a, F32) for a in (tq_c, tq_s, tk_c, tk_s))


def _mla_proj(p1f, n, gq, gkv, wq, wkv, tables):
    t = p1f.shape[0]
    tm = _pick(n, 512)
    per_seq = n // tm
    wide = HEADS * 2 * LANE
    tab_spec = pl.BlockSpec((tm, LANE), lambda i: (i % per_seq, 0))
    return pl.pallas_call(
        functools.partial(_mla_proj_kernel, scale=(MLA_NOPE + MLA_ROPE) ** -0.5 * np.log2(np.e)),
        grid=(t // tm,),
        in_specs=[pl.BlockSpec((tm, MLA_Q_LORA), lambda i: (i, BLK_C_Q * LANE // MLA_Q_LORA)),
                  pl.BlockSpec((tm, MLA_KV_LORA), lambda i: (i, BLK_C_KV * LANE // MLA_KV_LORA)),
                  pl.BlockSpec((tm, LANE), lambda i: (i, BLK_C_KR)),
                  pl.BlockSpec((1, MLA_Q_LORA), lambda i: (0, 0)),
                  pl.BlockSpec((1, MLA_KV_LORA), lambda i: (0, 0)),
                  pl.BlockSpec((MLA_Q_LORA, wide), lambda i: (0, 0)),
                  pl.BlockSpec((MLA_KV_LORA, wide), lambda i: (0, 0)),
                  tab_spec, tab_spec, tab_spec, tab_spec],
        out_specs=[pl.BlockSpec((tm, wide), lambda i: (i, 0)),
                   pl.BlockSpec((tm, wide), lambda i: (i, 0)),
                   pl.BlockSpec((tm, wide), lambda i: (i, 0))],
        out_shape=[jax.ShapeDtypeStruct((t, wide), CDT),
                   jax.ShapeDtypeStruct((t, wide), CDT),
                   jax.ShapeDtypeStruct((t, wide), CDT)],
        compiler_params=_cparams(("parallel",)),
        name="mla_projection",
    )(p1f, p1f, p1f, gq.reshape(1, -1), gkv.reshape(1, -1), wq, wkv, *tables)


def _mla_attn_kernel(q_ref, k_ref, v_ref, kc_ref, vc_ref, g_ref, o_ref, s_ref, *, n, nc, tk):
    q = q_ref[0]
    tq = q.shape[0]
    chunks = [(kc_ref, vc_ref, 0, nc, 0)] + [(k_ref, v_ref, c * tk, tk, nc + c * tk) for c in range(n // tk)]

    mv = jnp.full((tq, LANE), NEG, F32)
    for kr, _, start, size, col in chunks:
        s = _qk(q, kr[0, pl.ds(start, size), :])
        s_ref[:, col:col + size] = s
        for i in range(size // LANE):
            mv = jnp.maximum(mv, s[:, i * LANE:(i + 1) * LANE])
    m = jnp.max(mv, axis=-1, keepdims=True)

    acc = jnp.zeros((tq, 2 * LANE), F32)
    for _, vr, start, size, col in chunks:
        p = jnp.exp2(s_ref[:, col:col + size] - m).astype(CDT)
        acc = acc + jnp.dot(p, vr[0, pl.ds(start, size), :], preferred_element_type=F32)
    o = acc[:, :LANE] / acc[:, LANE:LANE + 1]
    o_ref[0] = (o * _silu(g_ref[0].astype(F32))).astype(o_ref.dtype)


def _mla_attention(q, k, v, kc, vc, p1, bsz, n, nc):
    tq = _pick(n, 512)
    tk = _pick(n, 512)
    kd = 2 * LANE
    return pl.pallas_call(
        functools.partial(_mla_attn_kernel, n=n, nc=nc, tk=tk),
        grid=(bsz, HEADS, n // tq),
        in_specs=[pl.BlockSpec((1, tq, kd), lambda b, h, i: (b, i, h)),
                  pl.BlockSpec((1, n, kd), lambda b, h, i: (b, 0, h)),
                  pl.BlockSpec((1, n, kd), lambda b, h, i: (b, 0, h)),
                  pl.BlockSpec((1, nc, kd), lambda b, h, i: (b, 0, h)),
                  pl.BlockSpec((1, nc, kd), lambda b, h, i: (b, 0, h)),
                  pl.BlockSpec((1, tq, LANE), lambda b, h, i: (b, i, BLK_C_GATE + h))],
        out_specs=pl.BlockSpec((1, tq, LANE), lambda b, h, i: (b, i, h)),
        out_shape=jax.ShapeDtypeStruct((bsz, n, BR_W), CDT),
        scratch_shapes=[pltpu.VMEM((tq, n + nc), F32)],
        compiler_params=_cparams(("parallel", "parallel", "arbitrary")),
        name="mla_attention",
    )(q, k, v, kc, vc, p1)


def _conv_kernel(x_ref, w_ref, o_ref, buf, *, n, k_scale):
    cb = pl.program_id(1)
    front = CONV_K // 2
    pad = 8
    buf[pl.ds(0, pad), :] = jnp.zeros((pad, LANE), F32)
    buf[pl.ds(pad + n, pad), :] = jnp.zeros((pad, LANE), F32)
    buf[pl.ds(pad, n), :] = x_ref[0].astype(F32)
    acc = None
    for j in range(CONV_K):
        term = buf[pl.ds(pad + j - front, n), :] * w_ref[j:j + 1, :]
        acc = term if acc is None else acc + term
    y = _silu(acc) * jnp.where(cb >= HEADS, k_scale, 1.0)
    o_ref[0] = y.astype(o_ref.dtype)


def _short_conv(p1, conv_w):
    bsz, n, _ = p1.shape
    ncb = 2 * HEADS
    return pl.pallas_call(
        functools.partial(_conv_kernel, n=n, k_scale=HEAD_DIM ** -0.5),
        grid=(bsz, ncb),
        in_specs=[pl.BlockSpec((1, n, LANE), lambda b, c: (b, 0, BLK_D_Q + c)),
                  pl.BlockSpec((CONV_K, LANE), lambda b, c: (0, c))],
        out_specs=pl.BlockSpec((1, n, LANE), lambda b, c: (b, 0, c)),
        out_shape=jax.ShapeDtypeStruct((bsz, n, 2 * BR_W), CDT),
        scratch_shapes=[pltpu.VMEM((n + 16, LANE), F32)],
        compiler_params=_cparams(("parallel", "arbitrary")),
        name="mlstm_conv",
    )(p1, conv_w)


def _scan_rows(x, op, fill, reverse):
    n = x.shape[0]
    t = lax.broadcasted_iota(jnp.int32, x.shape, 0)
    k = 1
    while k < n:
        if reverse:
            sh = jnp.where(t < n - k, pltpu.roll(x, n - k, 0), fill)
        else:
            sh = jnp.where(t >= k, pltpu.roll(x, k, 0), fill)
        x = op(x, sh)
        k *= 2
    return x


def _mlstm_kernel(qkc_ref, vc_ref, gc_ref, qkf_ref, vf_ref, gf_ref, qkb_ref, vb_ref, gb_ref, bif_ref,
                  hf_ref, hb_ref, c_ref, m_ref):
    j = pl.program_id(1)
    L = ML_CHUNK
    first = j == 0

    @pl.when(j == 0)
    def _():
        c_ref[...] = jnp.zeros(c_ref.shape, F32)
        m_ref[...] = jnp.zeros(m_ref.shape, F32)

    row = lax.broadcasted_iota(jnp.int32, (L, L), 0)
    col = lax.broadcasted_iota(jnp.int32, (L, L), 1)
    ones_col = jnp.where(lax.broadcasted_iota(jnp.int32, (L, LANE), 1) == 0, 1.0, 0.0).astype(CDT)

    for d, (qk_ref, v_ref, g_ref, h_ref) in enumerate(((qkf_ref, vf_ref, gf_ref, hf_ref),
                                                         (qkb_ref, vb_ref, gb_ref, hb_ref))):
        rev = d == 1
        qk_all = jnp.where(first, qkc_ref[0], qk_ref[0])
        v_all = jnp.where(first, vc_ref[0], v_ref[0])
        g = jnp.where(first, gc_ref[0], g_ref[0]) + bif_ref[...]
        li = g[:, :LANE]
        lf = jax.nn.log_sigmoid(g[:, LANE:])
        bc = _scan_rows(lf, jnp.add, 0.0, rev)
        r = li - bc
        cm = _scan_rows(r, jnp.maximum, NEG, rev)
        m_row = m_ref[d, 0:1, :]
        big_m = jnp.maximum(m_row, cm)
        a_all = jnp.exp(m_row - big_m)
        floor_all = jnp.exp(-(bc + big_m))
        m_fin = jnp.maximum(m_row, jnp.max(r, axis=0, keepdims=True))
        wk_all = jnp.exp(r - m_fin)
        decay_all = jnp.exp(m_row - m_fin)
        b_last = bc[0:1, :] if rev else bc[L - 1:L, :]
        m_ref[d, 0:1, :] = b_last + m_fin
        r_t = r.T
        causal = (col >= row) if rev else (col <= row)

        for h in range(HEADS):
            c = d * HEADS + h
            q = qk_all[:, h * LANE:(h + 1) * LANE]
            k = qk_all[:, (HEADS + h) * LANE:(HEADS + h + 1) * LANE]
            v_ext = jnp.concatenate([v_all[:, h * LANE:(h + 1) * LANE], ones_col], axis=1)
            dm = jnp.exp(jnp.where(causal, r_t[c:c + 1, :] - big_m[:, c:c + 1], NEG))
            s = (_qk(q, k) * dm).astype(CDT)
            c_old = c_ref[c]
            nd = (a_all[:, c:c + 1] * jnp.dot(q, c_old.astype(CDT), preferred_element_type=F32)
                  + jnp.dot(s, v_ext, preferred_element_type=F32))
            den = jnp.maximum(jnp.abs(nd[:, LANE:LANE + 1]), floor_all[:, c:c + 1])
            h_ref[0, :, h * LANE:(h + 1) * LANE] = nd[:, :LANE] / den
            kw_t = (k.astype(F32) * wk_all[:, c:c + 1]).T.astype(CDT)
            upd = jnp.dot(kw_t, v_ext, preferred_element_type=F32)
            c_ref[c] = decay_all[:, c:c + 1] * c_old + upd


def _mlstm_scan(qk_l, p1, g_l, qk_c, p1c, g_c, b_if2):
    L = ML_CHUNK
    bsz, n, _ = qk_l.shape
    nc = qk_c.shape[1]
    nl = n // L
    steps = nl + 1
    vblk = BLK_D_V * LANE // BR_W
    fwd = lambda j: jnp.maximum(j - 1, 0)
    bwd = lambda j: jnp.minimum(nl - j, nl - 1)
    return pl.pallas_call(
        _mlstm_kernel,
        grid=(bsz, steps),
        in_specs=[pl.BlockSpec((1, L, 2 * BR_W), lambda b, j: (b, 0, 0)),
                  pl.BlockSpec((1, L, BR_W), lambda b, j: (b, 0, vblk)),
                  pl.BlockSpec((1, L, 2 * LANE), lambda b, j: (b, 0, 0)),
                  pl.BlockSpec((1, L, 2 * BR_W), lambda b, j: (b, fwd(j), 0)),
                  pl.BlockSpec((1, L, BR_W), lambda b, j: (b, fwd(j), vblk)),
                  pl.BlockSpec((1, L, 2 * LANE), lambda b, j: (b, fwd(j), 0)),
                  pl.BlockSpec((1, L, 2 * BR_W), lambda b, j: (b, bwd(j), 0)),
                  pl.BlockSpec((1, L, BR_W), lambda b, j: (b, bwd(j), vblk)),
                  pl.BlockSpec((1, L, 2 * LANE), lambda b, j: (b, bwd(j), 0)),
                  pl.BlockSpec((1, 2 * LANE), lambda b, j: (0, 0))],
        out_specs=[pl.BlockSpec((1, L, BR_W), lambda b, j: (b, j, 0)),
                   pl.BlockSpec((1, L, BR_W), lambda b, j: (b, steps - 1 - j, 0))],
        out_shape=[jax.ShapeDtypeStruct((bsz, n + nc, BR_W), F32),
                   jax.ShapeDtypeStruct((bsz, n + nc, BR_W), F32)],
        scratch_shapes=[pltpu.VMEM((2 * HEADS, LANE, 2 * LANE), F32),
                        pltpu.VMEM((2, 8, LANE), F32)],
        compiler_params=_cparams(("parallel", "arbitrary")),
        name="mlstm_scan",
    )(qk_c, p1c, g_c, qk_l, p1, g_l, qk_l, p1, g_l, b_if2)


def _mlstm_out_kernel(hf_ref, hb_ref, o_ref, g_ref, gn_ref, y_ref):
    for h in range(HEADS):
        sl = slice(h * LANE, (h + 1) * LANE)
        hh = hf_ref[0, :, sl] + hb_ref[0, :, sl]
        hn = hh * lax.rsqrt(jnp.mean(hh * hh, axis=-1, keepdims=True) + EPS)
        y = hn * gn_ref[:, sl] * jax.nn.sigmoid(o_ref[0, :, sl].astype(F32))
        y_ref[0, :, sl] = (y * _silu(g_ref[0, :, sl].astype(F32))).astype(y_ref.dtype)


def _mlstm_output(h_f, h_b, p1, gnorm, f_off, b_off):
    bsz, n, _ = p1.shape
    L = ML_CHUNK
    wblk = BR_W // LANE
    return pl.pallas_call(
        _mlstm_out_kernel,
        grid=(bsz, n // L),
        in_specs=[pl.BlockSpec((1, L, BR_W), lambda b, i: (b, f_off + i, 0)),
                  pl.BlockSpec((1, L, BR_W), lambda b, i: (b, b_off + i, 0)),
                  pl.BlockSpec((1, L, BR_W), lambda b, i: (b, i, BLK_D_O // wblk)),
                  pl.BlockSpec((1, L, BR_W), lambda b, i: (b, i, BLK_D_GATE // wblk)),
                  pl.BlockSpec((1, BR_W), lambda b, i: (0, 0))],
        out_specs=pl.BlockSpec((1, L, BR_W), lambda b, i: (b, i, 0)),
        out_shape=jax.ShapeDtypeStruct((bsz, n, BR_W), CDT),
        compiler_params=_cparams(("parallel", "parallel")),
        name="mlstm_output",
    )(h_f, h_b, p1, p1, gnorm.reshape(1, BR_W))


def _merge_kernel(x_ref, mod_ref, ya, yb, yc, yd, mp_ref, wbr_ref, wout_ref, gfin_ref, o_ref, *, d, final_norm):
    acc = None
    for i, y_ref in enumerate((ya, yb, yc, yd)):
        z = jnp.dot(y_ref[...], wbr_ref[i], preferred_element_type=F32)
        t = jax.nn.sigmoid(mp_ref[:, i * d:(i + 1) * d].astype(F32)) * z
        acc = t if acc is None else acc + t
    out = jnp.dot(acc.astype(CDT), wout_ref[...], preferred_element_type=F32)
    x = x_ref[...] + mod_ref[0, 2:3, :] * out
    if final_norm:
        x = x * lax.rsqrt(jnp.mean(x * x, axis=-1, keepdims=True) + EPS) * gfin_ref[...]
    o_ref[...] = x


def _merge(x2, mod, mod_row, ys, mp, w_br, w_out, g_final, final_norm, tm):
    t, d = x2.shape
    y_spec = pl.BlockSpec((tm, BR_W), lambda i: (i, 0))
    return pl.pallas_call(
        functools.partial(_merge_kernel, d=d, final_norm=final_norm),
        grid=(t // tm,),
        in_specs=[pl.BlockSpec((tm, d), lambda i: (i, 0)),
                  pl.BlockSpec((1, 3, d), lambda i: (mod_row(i), 0, 0)),
                  y_spec, y_spec, y_spec, y_spec,
                  pl.BlockSpec((tm, N_BRANCH * d), lambda i: (i, 0)),
                  pl.BlockSpec((N_BRANCH, BR_W, d), lambda i: (0, 0, 0), pipeline_mode=pl.Buffered(1)),
                  pl.BlockSpec((d, d), lambda i: (0, 0), pipeline_mode=pl.Buffered(1)),
                  pl.BlockSpec((1, d), lambda i: (0, 0))],
        out_specs=pl.BlockSpec((tm, d), lambda i: (i, 0)),
        out_shape=jax.ShapeDtypeStruct((t, d), F32),
        compiler_params=_cparams(("parallel",)),
        name="merge",
    )(x2, mod, *ys, mp, w_br, w_out, g_final.reshape(1, d))


def _split_offsets(d):
    widths = (3 * BR_W, BR_W, BR_W, BR_W, MLA_Q_LORA, MLA_KV_LORA, MLA_ROPE, BR_W,
              3 * BR_W, BR_W, 4 * HEADS, BR_W, N_BRANCH * d)
    names = ('a_qkv', 'a_gate', 'b_in', 'b_gate', 'c_q', 'c_kv', 'c_kr', 'c_gate',
             'd_qkv', 'd_o', 'd_if', 'd_gate', 'merge')
    offs = np.concatenate([[0], np.cumsum(widths)])
    return {nm: (int(offs[i]), int(offs[i + 1])) for i, nm in enumerate(names)}


def _rope_swap_perm():
    half = MLA_ROPE // 2
    q = half // 2
    return np.concatenate([np.arange(q, half), np.arange(0, q), np.arange(half + q, MLA_ROPE), np.arange(half, half + q)])


def _prep_w_in(w_in):
    nl, d, _ = w_in.shape
    o = _split_offsets(d)
    cut = lambda nm: w_in[:, :, o[nm][0]:o[nm][1]]
    kr = cut('c_kr')
    d_if = cut('d_if')
    d_if = d_if.reshape(nl, d, 2, 2, HEADS)
    zpad = jnp.zeros((nl, d, LANE - 2 * HEADS), w_in.dtype)
    li = jnp.concatenate([d_if[:, :, :, 0, :].reshape(nl, d, 2 * HEADS), zpad], axis=-1)
    lf = jnp.concatenate([d_if[:, :, :, 1, :].reshape(nl, d, 2 * HEADS), zpad], axis=-1)
    w1 = jnp.concatenate([
        w_in[:, :, :o['c_kv'][1]], cut('c_gate'), cut('d_qkv'), cut('d_o'), cut('d_gate'),
        kr, kr[:, :, _rope_swap_perm()], jnp.zeros((nl, d, (BLK_D_LI - BLK_C_KR - 1) * LANE), w_in.dtype),
        li, lf], axis=-1)
    assert w1.shape[-1] == N1, w1.shape
    return w1.astype(CDT), cut('merge').astype(CDT)


def _prep_b_if(b_if):
    nl = b_if.shape[0]
    b = b_if.reshape(nl, 2, 2, HEADS)
    zpad = jnp.zeros((nl, LANE - 2 * HEADS), b_if.dtype)
    li = jnp.concatenate([b[:, :, 0, :].reshape(nl, 2 * HEADS), zpad], axis=-1)
    lf = jnp.concatenate([b[:, :, 1, :].reshape(nl, 2 * HEADS), zpad], axis=-1)
    return jnp.concatenate([li, lf], axis=-1).reshape(nl, 1, 2 * LANE)


def _prep_w_uq(w_uq):
    nl, r, _ = w_uq.shape
    w = w_uq.reshape(nl, r, HEADS, MLA_NOPE + MLA_ROPE)
    rope = w[..., MLA_NOPE:]
    w = jnp.concatenate([w[..., :MLA_NOPE], rope, rope[..., _rope_swap_perm()]], axis=-1)
    return w.reshape(nl, r, HEADS * 2 * LANE).astype(CDT)


def kernel(x, c, ctx, c_ctx, norm_g, w_mod, b_mod, w_in, na_rpb, pool_w, pool_scale, mla_gq, mla_gkv, w_uq, w_ukv,
           conv_w, b_if, ml_gnorm, w_br, w_out, g_final):
    bsz, n, d = x.shape
    nc = ctx.shape[1]
    depth = w_in.shape[0]
    rows = n // GRID_W
    assert n % (NA_QROWS * GRID_W) == 0 and rows >= NA_KROWS
    assert nc == ML_CHUNK and n % ML_CHUNK == 0 and nc % 8 == 0

    w1_all, w2_all = _prep_w_in(w_in)
    wq_all = _prep_w_uq(w_uq)
    wkv_all = w_ukv.astype(CDT)
    wbr_all = w_br.astype(CDT)
    wout_all = w_out.astype(CDT)
    poolw_all = pool_w.astype(CDT)
    bif_all = _prep_b_if(b_if)
    bias_all = _na_column_tables(na_rpb)
    tab_lat = _rope_tables(n, False)
    tab_ctx = _rope_tables(nc, True)

    n_cond = -(-(bsz + 1) // 8) * 8
    cond = jnp.concatenate([c, c_ctx[None, :], jnp.zeros((n_cond - bsz - 1, d), F32)], axis=0)
    mods = _modulation(cond, w_mod, b_mod)
    mods = mods.reshape(depth, n_cond, 3, d)

    tm_l = _pick(n, 1024)
    tm_c = _pick(bsz * nc, 1024)
    tn1 = _pick(N1, 1536) if N1 % 1536 == 0 else _pick(N1, 512)
    tn2 = _pick(N_BRANCH * d, 1024)
    tm_merge = _pick(n, 256)
    lat_row = lambda i: i // (n // tm_l)
    ctx_row = lambda i: bsz
    lat_row_m = lambda i: i // (n // tm_merge)

    xl = x.reshape(bsz * n, d)
    xc = ctx.reshape(bsz * nc, d)
    for l in range(depth):
        need_ctx = l < depth - 1
        last = l == depth - 1
        mod = mods[l]
        p1, gates = _inproj(xl, mod, lat_row, norm_g[l], w1_all[l], tm_l, tn1, 2 * LANE)
        p2 = _inproj(xl, mod, lat_row, norm_g[l], w2_all[l], tm_l, tn2, 0)[0]
        p1c, gates_c = _inproj(xc, mod, ctx_row, norm_g[l], w1_all[l], tm_c, tn1, 2 * LANE)
        p1_3 = p1.reshape(bsz, n, N1)
        p1c_3 = p1c.reshape(bsz, nc, N1)

        ya = _neighbourhood(p1_3, p1c_3, bias_all[l], bsz, n, nc)
        yb = _pool(p1_3, poolw_all[l], pool_scale[l])
        q_l, k_l, v_l = _mla_proj(p1, n, mla_gq[l], mla_gkv[l], wq_all[l], wkv_all[l], tab_lat)
        q_c, k_c, v_c = _mla_proj(p1c, nc, mla_gq[l], mla_gkv[l], wq_all[l], wkv_all[l], tab_ctx)
        r3 = lambda a, m: a.reshape(bsz, m, a.shape[-1])
        yc = _mla_attention(r3(q_l, n), r3(k_l, n), r3(v_l, n), r3(k_c, nc), r3(v_c, nc), p1_3, bsz, n, nc)
        qk_l = _short_conv(p1_3, conv_w[l])
        qk_c = _short_conv(p1c_3, conv_w[l])
        g_l = gates.reshape(bsz, n, 2 * LANE)
        g_c = gates_c.reshape(bsz, nc, 2 * LANE)
        h_f, h_b = _mlstm_scan(qk_l, p1_3, g_l, qk_c, p1c_3, g_c, bif_all[l])
        yd = _mlstm_output(h_f, h_b, p1_3, ml_gnorm[l], nc // ML_CHUNK, 0)

        f2 = lambda a: a.reshape(-1, a.shape[-1])
        xl_new = _merge(xl, mod, lat_row_m, (f2(ya), f2(yb), f2(yc), f2(yd)), p2, wbr_all[l], wout_all[l],
                        g_final, last, tm_merge)
        if need_ctx:
            p2c = _inproj(xc, mod, ctx_row, norm_g[l], w2_all[l], tm_c, tn2, 0)[0]
            ya_c = _ctx_attention(p1c_3, BLK_A_Q, LANE, p1c_3, BLK_A_K, LANE, p1c_3, BLK_A_V, 1, p1c_3, BLK_A_GATE,
                                  HEAD_DIM ** -0.5, False)
            yb_c = _pool(p1c_3, poolw_all[l], pool_scale[l])
            yc_c = _ctx_attention(r3(q_c, nc), 0, 2 * LANE, r3(k_c, nc), 0, 2 * LANE, r3(v_c, nc), 0, 2,
                                  p1c_3, BLK_C_GATE, 1.0, True)
            yd_c = _mlstm_output(h_f, h_b, p1c_3, ml_gnorm[l], 0, n // ML_CHUNK)
            xc = _merge(xc, mod, ctx_row, (f2(ya_c), f2(yb_c), f2(yc_c), f2(yd_c)), p2c, wbr_all[l], wout_all[l],
                        g_final, False, _pick(nc * bsz, 256))
        xl = xl_new
    return xl.reshape(bsz, n, d)
```

```python
import functools

import numpy as np
import jax
import jax.numpy as jnp
from jax import lax
from jax.experimental import pallas as pl
from jax.experimental.pallas import tpu as pltpu

F32 = jnp.float32
CDT = jnp.bfloat16

GRID_W = 64
EPS = 1e-6
N_BRANCH = 4
BR_W = 512
HEADS = 4
HEAD_DIM = BR_W // HEADS
NA_ROWS = 8
NA_COLS = 16
POOL_WINDOWS = (2, 4, 8, 16)
MLA_NOPE = 128
MLA_ROPE = 64
MLA_Q_LORA = 512
MLA_KV_LORA = 512
ROPE_THETA = 10000.0
CONV_K = 4

LANE = 128
VMEM_LIMIT = 56 * 1024 * 1024
NEG = -1e30

BLK_A_Q, BLK_A_K, BLK_A_V, BLK_A_GATE = 0, 4, 8, 12
BLK_B_IN, BLK_B_GATE = 16, 20
BLK_C_Q, BLK_C_KV, BLK_C_GATE = 24, 28, 32
BLK_D_Q, BLK_D_K, BLK_D_V, BLK_D_O, BLK_D_GATE = 36, 40, 44, 48, 52
BLK_C_KR = 56
BLK_D_LI, BLK_D_LF = 58, 59
N1_BLOCKS = 60
N1 = N1_BLOCKS * LANE

NA_QROWS = 8
NA_KROWS = 16
ML_CHUNK = 256
MLA_ROW_SPLIT = 2


def _cparams(sem):
    return pltpu.CompilerParams(dimension_semantics=sem, vmem_limit_bytes=VMEM_LIMIT)


def _silu(x):
    return x * jax.nn.sigmoid(x)


def _pick(n, pref):
    t = min(n, pref)
    while n % t:
        t //= 2
    return t


def _mod_kernel(c_ref, w_ref, b_ref, o_ref):
    c = _silu(c_ref[...]).astype(CDT)
    o_ref[0] = jnp.dot(c, w_ref[0].astype(CDT), preferred_element_type=F32) + b_ref[0]


def _modulation(cond, w_mod, b_mod):
    nl, d, d3 = w_mod.shape
    r = cond.shape[0]
    tn = _pick(d3, 1024)
    return pl.pallas_call(
        _mod_kernel,
        grid=(nl, d3 // tn),
        in_specs=[pl.BlockSpec((r, d), lambda l, j: (0, 0)),
                  pl.BlockSpec((1, d, tn), lambda l, j: (l, 0, j)),
                  pl.BlockSpec((1, 1, tn), lambda l, j: (l, 0, j))],
        out_specs=pl.BlockSpec((1, r, tn), lambda l, j: (l, 0, j)),
        out_shape=jax.ShapeDtypeStruct((nl, r, d3), F32),
        compiler_params=_cparams(("parallel", "parallel")),
        name="modulation",
    )(cond, w_mod, b_mod.reshape(nl, 1, d3))


def _inproj_kernel(x_ref, mod_ref, g_ref, w_ref, o_ref, *rest, n_f32_cols):
    hn_ref = rest[-1]
    j = pl.program_id(1)

    @pl.when(j == 0)
    def _():
        x = x_ref[...]
        y = x * lax.rsqrt(jnp.mean(x * x, axis=-1, keepdims=True) + EPS) * g_ref[...]
        hn = y * (1.0 + mod_ref[0, 1:2, :]) + mod_ref[0, 0:1, :]
        hn_ref[...] = hn.astype(CDT)

    acc = jnp.dot(hn_ref[...], w_ref[...], preferred_element_type=F32)
    o_ref[...] = acc.astype(o_ref.dtype)
    if n_f32_cols:
        f_ref = rest[0]

        @pl.when(j == pl.num_programs(1) - 1)
        def _():
            f_ref[...] = acc[:, acc.shape[1] - n_f32_cols:]


def _inproj(x2, mod, mod_row, norm_g, w, tm, tn, n_f32_cols):
    t, d = x2.shape
    n = w.shape[1]
    out_shape = [jax.ShapeDtypeStruct((t, n), CDT)]
    out_specs = [pl.BlockSpec((tm, tn), lambda i, j: (i, j))]
    if n_f32_cols:
        out_shape.append(jax.ShapeDtypeStruct((t, n_f32_cols), F32))
        out_specs.append(pl.BlockSpec((tm, n_f32_cols), lambda i, j: (i, 0)))
    res = pl.pallas_call(
        functools.partial(_inproj_kernel, n_f32_cols=n_f32_cols),
        grid=(t // tm, n // tn),
        in_specs=[pl.BlockSpec((tm, d), lambda i, j: (i, 0)),
                  pl.BlockSpec((1, 3, d), lambda i, j: (mod_row(i), 0, 0)),
                  pl.BlockSpec((1, d), lambda i, j: (0, 0)),
                  pl.BlockSpec((d, tn), lambda i, j: (0, j))],
        out_specs=out_specs,
        out_shape=out_shape,
        scratch_shapes=[pltpu.VMEM((tm, d), CDT)],
        compiler_params=_cparams(("parallel", "arbitrary")),
        name="inproj",
    )(x2, mod, norm_g.reshape(1, d), w)
    return res


def _softmax_pv(parts, exp=jnp.exp):
    m = None
    for s, _ in parts:
        mi = jnp.max(s, axis=-1, keepdims=True)
        m = mi if m is None else jnp.maximum(m, mi)
    l = None
    o = None
    for s, v in parts:
        p = exp(s - m)
        li = jnp.sum(p, axis=-1, keepdims=True)
        oi = jnp.dot(p.astype(CDT), v, preferred_element_type=F32)
        l = li if l is None else l + li
        o = oi if o is None else o + oi
    return o / l


def _qk(q, k):
    return lax.dot_general(q, k, (((1,), (1,)), ((), ())), preferred_element_type=F32)


def _na_row_offsets(rows, jt):
    start = int(np.clip(jt * NA_QROWS - NA_ROWS // 2, 0, rows - NA_KROWS))
    r = jt * NA_QROWS + np.arange(NA_QROWS)
    keyrow = start + np.arange(NA_KROWS)
    rs = np.clip(r - NA_ROWS // 2, 0, rows - NA_ROWS)
    row_ok = (keyrow[None, :] >= rs[:, None]) & (keyrow[None, :] < rs[:, None] + NA_ROWS)
    return np.where(row_ok, keyrow[None, :] - r[:, None] + NA_ROWS - 1, -1)


def _na_kernel(q_ref, k_ref, v_ref, kc_ref, vc_ref, g_ref, bc_ref, o_ref, bias_ref, *, rows, scale):
    j = pl.program_id(2)
    nj = rows // NA_QROWS
    w = GRID_W
    left = lax.broadcasted_iota(jnp.int32, (w, LANE), 1) < w

    def build(jt):
        roff = _na_row_offsets(rows, jt)
        neg = jnp.full((w, LANE), NEG, F32)
        for g in range(NA_QROWS):
            for kp in range(NA_KROWS // 2):
                ra, rb = int(roff[g, 2 * kp]), int(roff[g, 2 * kp + 1])
                ta = bc_ref[0, ra] if ra >= 0 else neg
                tb = bc_ref[0, rb] if rb >= 0 else neg
                tile = neg if (ra < 0 and rb < 0) else jnp.where(left, ta, tb)
                bias_ref[g * w:(g + 1) * w, kp * LANE:(kp + 1) * LANE] = tile

    pl.when(j == 0)(lambda: build(0))
    if nj > 2:
        pl.when(j == 1)(lambda: build(1))
    pl.when(j == nj - 1)(lambda: build(nj - 1))

    start = jnp.clip(j * NA_QROWS - NA_ROWS // 2, 0, rows - NA_KROWS) * GRID_W
    start = pl.multiple_of(start, GRID_W)
    kw = k_ref[0, pl.ds(start, NA_KROWS * GRID_W), :]
    vw = v_ref[0, pl.ds(start, NA_KROWS * GRID_W), :]
    half = q_ref.shape[1] // 2
    for rs in (slice(0, half), slice(half, 2 * half)):
        q = q_ref[0, rs, :]
        s_loc = _qk(q, kw) * scale + bias_ref[rs, :]
        s_ctx = _qk(q, kc_ref[0]) * scale
        o = _softmax_pv([(s_loc, vw), (s_ctx, vc_ref[0])])
        o_ref[0, rs, :] = (o * _silu(g_ref[0, rs, :].astype(F32))).astype(o_ref.dtype)


def _na_column_tables(rpb):
    w = GRID_W
    nl, nh, n_ro, n_co = rpb.shape
    cq = np.arange(w)
    cstart = np.clip(cq - NA_COLS // 2, 0, w - NA_COLS)
    col_ok = (cq[None, :] >= cstart[:, None]) & (cq[None, :] < cstart[:, None] + NA_COLS)
    period = 2 * w + 1
    v = jnp.concatenate([rpb[..., NA_COLS - 1:], jnp.zeros((nl, nh, n_ro, period - n_co), rpb.dtype),
                         rpb[..., :NA_COLS - 1]], axis=-1).astype(F32)
    t = jnp.tile(v, (1, 1, 1, w))[..., :w * 2 * w].reshape(nl, nh, n_ro, w, 2 * w)[..., :w]
    t = jnp.where(jnp.asarray(col_ok), t, NEG)
    return jnp.concatenate([t, t], axis=-1)


def _neighbourhood(p1, p1c, bc, bsz, n, nc):
    rows = n // GRID_W
    nj = rows // NA_QROWS
    tq = NA_QROWS * GRID_W
    n_ro = bc.shape[1]
    return pl.pallas_call(
        functools.partial(_na_kernel, rows=rows, scale=HEAD_DIM ** -0.5),
        grid=(bsz, HEADS, nj),
        in_specs=[pl.BlockSpec((1, tq, LANE), lambda b, h, j: (b, j, BLK_A_Q + h)),
                  pl.BlockSpec((1, n, LANE), lambda b, h, j: (b, 0, BLK_A_K + h)),
                  pl.BlockSpec((1, n, LANE), lambda b, h, j: (b, 0, BLK_A_V + h)),
                  pl.BlockSpec((1, nc, LANE), lambda b, h, j: (b, 0, BLK_A_K + h)),
                  pl.BlockSpec((1, nc, LANE), lambda b, h, j: (b, 0, BLK_A_V + h)),
                  pl.BlockSpec((1, tq, LANE), lambda b, h, j: (b, j, BLK_A_GATE + h)),
                  pl.BlockSpec((1, n_ro, GRID_W, 2 * GRID_W), lambda b, h, j: (h, 0, 0, 0))],
        out_specs=pl.BlockSpec((1, tq, LANE), lambda b, h, j: (b, j, h)),
        out_shape=jax.ShapeDtypeStruct((bsz, n, BR_W), CDT),
        scratch_shapes=[pltpu.VMEM((tq, NA_KROWS * GRID_W), F32)],
        compiler_params=_cparams(("parallel", "parallel", "arbitrary")),
        name="neighbourhood_attention",
    )(p1, p1, p1, p1c, p1c, p1, bc)


def _ctx_attn_kernel(q_ref, k_ref, v_ref, g_ref, o_ref, *, scale, base2):
    s = _qk(q_ref[0], k_ref[0]) * scale
    o = _softmax_pv([(s, v_ref[0])], jnp.exp2 if base2 else jnp.exp)
    o_ref[0] = (o * _silu(g_ref[0].astype(F32))).astype(o_ref.dtype)


def _ctx_attention(q, qblk, qw, k, kblk, kw, v, vblk, vstep, gate, gblk, scale, base2):
    bsz, nc, _ = q.shape
    return pl.pallas_call(
        functools.partial(_ctx_attn_kernel, scale=scale, base2=base2),
        grid=(bsz, HEADS),
        in_specs=[pl.BlockSpec((1, nc, qw), lambda b, h: (b, 0, qblk + h)),
                  pl.BlockSpec((1, nc, kw), lambda b, h: (b, 0, kblk + h)),
                  pl.BlockSpec((1, nc, LANE), lambda b, h: (b, 0, vblk + vstep * h)),
                  pl.BlockSpec((1, nc, LANE), lambda b, h: (b, 0, gblk + h))],
        out_specs=pl.BlockSpec((1, nc, LANE), lambda b, h: (b, 0, h)),
        out_shape=jax.ShapeDtypeStruct((bsz, nc, BR_W), CDT),
        compiler_params=_cparams(("parallel", "parallel")),
        name="context_attention",
    )(q, k, v, gate)


POOL_PAD = 16
POOL_HALO = 8


def _pool_kernel(u_ref, g_ref, w_ref, sc_ref, o_ref, buf_a, buf_b, *, n):
    gi = pl.program_id(1)
    x = u_ref[0].astype(F32)
    t = lax.broadcasted_iota(jnp.int32, (n, LANE), 0)
    ext = n + 2 * POOL_HALO

    def finish(wsum, w):
        lo = jnp.maximum(t - w // 2, 0)
        hi = jnp.minimum(t + (w - 1 - w // 2), n - 1)
        cnt = (hi - lo + 1).astype(F32)
        d = wsum / cnt - x
        y = jnp.dot(d.astype(CDT), w_ref[0], preferred_element_type=F32) * sc_ref[...]
        o_ref[0] = (y * _silu(g_ref[0].astype(F32))).astype(o_ref.dtype)

    def window(ref, off):
        return ref[pl.ds(POOL_PAD + off, n), :]

    def wide(ref, off):
        return ref[pl.ds(POOL_PAD - POOL_HALO + off, ext), :]

    for buf in (buf_a, buf_b):
        buf[pl.ds(0, POOL_PAD), :] = jnp.zeros((POOL_PAD, LANE), F32)
        buf[pl.ds(POOL_PAD + n, POOL_PAD), :] = jnp.zeros((POOL_PAD, LANE), F32)
    buf_a[pl.ds(POOL_PAD, n), :] = x

    for level, w in enumerate(POOL_WINDOWS):
        @pl.when(gi == level)
        def _(level=level, w=w):
            if level == 0:
                finish(window(buf_a, -1) + window(buf_a, 0), w)
                return
            buf_b[pl.ds(POOL_PAD - POOL_HALO, ext), :] = wide(buf_a, -1) + wide(buf_a, 0)
            src, dst = buf_b, buf_a
            for lv in range(1, level):
                h = 1 << (lv - 1)
                dst[pl.ds(POOL_PAD - POOL_HALO, ext), :] = wide(src, -h) + wide(src, h)
                src, dst = dst, src
            h = 1 << (level - 1)
            finish(window(src, -h) + window(src, h), w)


def _pool(p1, pool_w, pool_scale):
    bsz, n, _ = p1.shape
    ng = len(POOL_WINDOWS)
    return pl.pallas_call(
        functools.partial(_pool_kernel, n=n),
        grid=(bsz, ng),
        in_specs=[pl.BlockSpec((1, n, LANE), lambda b, g: (b, 0, BLK_B_IN + g)),
                  pl.BlockSpec((1, n, LANE), lambda b, g: (b, 0, BLK_B_GATE + g)),
                  pl.BlockSpec((1, LANE, LANE), lambda b, g: (g, 0, 0)),
                  pl.BlockSpec((1, LANE), lambda b, g: (0, g))],
        out_specs=pl.BlockSpec((1, n, LANE), lambda b, g: (b, 0, g)),
        out_shape=jax.ShapeDtypeStruct((bsz, n, BR_W), CDT),
        scratch_shapes=[pltpu.VMEM((n + 2 * POOL_PAD, LANE), F32),
                        pltpu.VMEM((n + 2 * POOL_PAD, LANE), F32)],
        compiler_params=_cparams(("parallel", "arbitrary")),
        name="pool_branch",
    )(p1, p1, pool_w, pool_scale.reshape(1, BR_W))


def _rms(x, g):
    return x * lax.rsqrt(jnp.mean(x * x, axis=-1, keepdims=True) + EPS) * g


def _mla_proj_kernel(cq_ref, ckv_ref, kr_ref, gq_ref, gkv_ref, wq_ref, wkv_ref,
                     tq_c, tq_s, tk_c, tk_s, q_ref, k_ref, v_ref, *, scale):
    hq = _rms(cq_ref[...].astype(F32), gq_ref[...]).astype(CDT)
    hkv = _rms(ckv_ref[...].astype(F32), gkv_ref[...]).astype(CDT)
    q = jnp.dot(hq, wq_ref[...], preferred_element_type=F32) * scale
    kv = jnp.dot(hkv, wkv_ref[...], preferred_element_type=F32)
    kr = kr_ref[...].astype(F32)
    k_rot = (kr * tk_c[...] + pltpu.roll(kr, LANE // 2, 1) * tk_s[...]).astype(CDT)
    ones_col = jnp.where(lax.broadcasted_iota(jnp.int32, kr.shape, 1) == 0, 1.0, 0.0).astype(CDT)
    for h in range(HEADS):
        lo = 2 * h * LANE
        q_rope = q[:, lo + LANE:lo + 2 * LANE]
        q_rot = q_rope * tq_c[...] + pltpu.roll(q_rope, LANE // 2, 1) * tq_s[...]
        q_ref[:, lo:lo + LANE] = q[:, lo:lo + LANE].astype(CDT)
        q_ref[:, lo + LANE:lo + 2 * LANE] = q_rot.astype(CDT)
        k_ref[:, lo:lo + LANE] = kv[:, lo:lo + LANE].astype(CDT)
        k_ref[:, lo + LANE:lo + 2 * LANE] = k_rot
        v_ref[:, lo:lo + LANE] = kv[:, lo + LANE:lo + 2 * LANE].astype(CDT)
        v_ref[:, lo + LANE:lo + 2 * LANE] = ones_col


def _rope_tables(n, is_ctx):
    half = MLA_ROPE // 2
    nfreq = half // 2
    inv = ROPE_THETA ** (-np.arange(nfreq, dtype=np.float64) / nfreq)
    pos = np.arange(n)
    ang_r = (pos // GRID_W)[:, None] * inv
    ang_c = (pos % GRID_W)[:, None] * inv
    cos = np.concatenate([np.cos(ang_r), np.cos(ang_r), np.cos(ang_c), np.cos(ang_c)], axis=1)
    sin = np.concatenate([-np.sin(ang_r), np.sin(ang_r), -np.sin(ang_c), np.sin(ang_c)], axis=1)
    zero = np.zeros_like(cos)
    one = np.ones_like(cos)
    tq_c = np.concatenate([cos, zero], axis=1)
    tq_s = np.concatenate([sin, one], axis=1)
    if is_ctx:
        tk_c = np.concatenate([zero, zero], axis=1)
        tk_s = np.concatenate([zero, one], axis=1)
    else:
        tk_c = np.concatenate([cos, zero], axis=1)
        tk_s = np.concatenate([sin, zero], axis=1)
    return tuple(jnp.asarray(a, F32) for a in (tq_c, tq_s, tk_c, tk_s))


def _mla_proj(p1f, n, gq, gkv, wq, wkv, tables):
    t = p1f.shape[0]
    tm = _pick(n, 512)
    per_seq = n // tm
    wide = HEADS * 2 * LANE
    tab_spec = pl.BlockSpec((tm, LANE), lambda i: (i % per_seq, 0))
    return pl.pallas_call(
        functools.partial(_mla_proj_kernel, scale=(MLA_NOPE + MLA_ROPE) ** -0.5 * np.log2(np.e)),
        grid=(t // tm,),
        in_specs=[pl.BlockSpec((tm, MLA_Q_LORA), lambda i: (i, BLK_C_Q * LANE // MLA_Q_LORA)),
                  pl.BlockSpec((tm, MLA_KV_LORA), lambda i: (i, BLK_C_KV * LANE // MLA_KV_LORA)),
                  pl.BlockSpec((tm, LANE), lambda i: (i, BLK_C_KR)),
                  pl.BlockSpec((1, MLA_Q_LORA), lambda i: (0, 0)),
                  pl.BlockSpec((1, MLA_KV_LORA), lambda i: (0, 0)),
                  pl.BlockSpec((MLA_Q_LORA, wide), lambda i: (0, 0)),
                  pl.BlockSpec((MLA_KV_LORA, wide), lambda i: (0, 0)),
                  tab_spec, tab_spec, tab_spec, tab_spec],
        out_specs=[pl.BlockSpec((tm, wide), lambda i: (i, 0)),
                   pl.BlockSpec((tm, wide), lambda i: (i, 0)),
                   pl.BlockSpec((tm, wide), lambda i: (i, 0))],
        out_shape=[jax.ShapeDtypeStruct((t, wide), CDT),
                   jax.ShapeDtypeStruct((t, wide), CDT),
                   jax.ShapeDtypeStruct((t, wide), CDT)],
        compiler_params=_cparams(("parallel",)),
        name="mla_projection",
    )(p1f, p1f, p1f, gq.reshape(1, -1), gkv.reshape(1, -1), wq, wkv, *tables)


def _mla_attn_kernel(q_ref, k_ref, v_ref, kc_ref, vc_ref, g_ref, o_ref, s_ref, *, n, nc, tk):
    tq = q_ref.shape[1]
    chunks = [(kc_ref, vc_ref, 0, nc, 0)] + [(k_ref, v_ref, c * tk, tk, nc + c * tk) for c in range(n // tk)]

    for rows in (slice(r, r + tq // MLA_ROW_SPLIT) for r in range(0, tq, tq // MLA_ROW_SPLIT)):
        q = q_ref[0, rows, :]
        mv = jnp.full((q.shape[0], LANE), NEG, F32)
        for kr, _, start, size, col in chunks:
            s = _qk(q, kr[0, pl.ds(start, size), :])
            s_ref[rows, col:col + size] = s
            for i in range(size // LANE):
                mv = jnp.maximum(mv, s[:, i * LANE:(i + 1) * LANE])
        m = jnp.max(mv, axis=-1, keepdims=True)

        acc = jnp.zeros((q.shape[0], 2 * LANE), F32)
        for _, vr, start, size, col in chunks:
            p = jnp.exp2(s_ref[rows, col:col + size] - m).astype(CDT)
            acc = acc + jnp.dot(p, vr[0, pl.ds(start, size), :], preferred_element_type=F32)
        o = acc[:, :LANE] / acc[:, LANE:LANE + 1]
        o_ref[0, rows, :] = (o * _silu(g_ref[0, rows, :].astype(F32))).astype(o_ref.dtype)


def _mla_attention(q, k, v, kc, vc, p1, bsz, n, nc):
    tq = _pick(n, 512)
    tk = _pick(n, 512)
    kd = 2 * LANE
    return pl.pallas_call(
        functools.partial(_mla_attn_kernel, n=n, nc=nc, tk=tk),
        grid=(bsz, HEADS, n // tq),
        in_specs=[pl.BlockSpec((1, tq, kd), lambda b, h, i: (b, i, h)),
                  pl.BlockSpec((1, n, kd), lambda b, h, i: (b, 0, h)),
                  pl.BlockSpec((1, n, kd), lambda b, h, i: (b, 0, h)),
                  pl.BlockSpec((1, nc, kd), lambda b, h, i: (b, 0, h)),
                  pl.BlockSpec((1, nc, kd), lambda b, h, i: (b, 0, h)),
                  pl.BlockSpec((1, tq, LANE), lambda b, h, i: (b, i, BLK_C_GATE + h))],
        out_specs=pl.BlockSpec((1, tq, LANE), lambda b, h, i: (b, i, h)),
        out_shape=jax.ShapeDtypeStruct((bsz, n, BR_W), CDT),
        scratch_shapes=[pltpu.VMEM((tq, n + nc), F32)],
        compiler_params=_cparams(("parallel", "parallel", "arbitrary")),
        name="mla_attention",
    )(q, k, v, kc, vc, p1)


def _conv_kernel(x_ref, w_ref, o_ref, buf, *, n, k_scale):
    cb = pl.program_id(1)
    front = CONV_K // 2
    pad = 8
    buf[pl.ds(0, pad), :] = jnp.zeros((pad, LANE), F32)
    buf[pl.ds(pad + n, pad), :] = jnp.zeros((pad, LANE), F32)
    buf[pl.ds(pad, n), :] = x_ref[0].astype(F32)
    acc = None
    for j in range(CONV_K):
        term = buf[pl.ds(pad + j - front, n), :] * w_ref[j:j + 1, :]
        acc = term if acc is None else acc + term
    y = _silu(acc) * jnp.where(cb >= HEADS, k_scale, 1.0)
    o_ref[0] = y.astype(o_ref.dtype)


def _short_conv(p1, conv_w):
    bsz, n, _ = p1.shape
    ncb = 2 * HEADS
    return pl.pallas_call(
        functools.partial(_conv_kernel, n=n, k_scale=HEAD_DIM ** -0.5),
        grid=(bsz, ncb),
        in_specs=[pl.BlockSpec((1, n, LANE), lambda b, c: (b, 0, BLK_D_Q + c)),
                  pl.BlockSpec((CONV_K, LANE), lambda b, c: (0, c))],
        out_specs=pl.BlockSpec((1, n, LANE), lambda b, c: (b, 0, c)),
        out_shape=jax.ShapeDtypeStruct((bsz, n, 2 * BR_W), CDT),
        scratch_shapes=[pltpu.VMEM((n + 16, LANE), F32)],
        compiler_params=_cparams(("parallel", "arbitrary")),
        name="mlstm_conv",
    )(p1, conv_w)


def _scan_rows(x, op, fill, reverse):
    n = x.shape[0]
    t = lax.broadcasted_iota(jnp.int32, x.shape, 0)
    k = 1
    while k < n:
        if reverse:
            sh = jnp.where(t < n - k, pltpu.roll(x, n - k, 0), fill)
        else:
            sh = jnp.where(t >= k, pltpu.roll(x, k, 0), fill)
        x = op(x, sh)
        k *= 2
    return x


def _mlstm_kernel(qkc_ref, vc_ref, gc_ref, qkf_ref, vf_ref, gf_ref, qkb_ref, vb_ref, gb_ref, bif_ref,
                  hf_ref, hb_ref, c_ref, m_ref):
    j = pl.program_id(1)
    L = ML_CHUNK
    first = j == 0

    @pl.when(j == 0)
    def _():
        c_ref[...] = jnp.zeros(c_ref.shape, F32)
        m_ref[...] = jnp.zeros(m_ref.shape, F32)

    row = lax.broadcasted_iota(jnp.int32, (L, L), 0)
    col = lax.broadcasted_iota(jnp.int32, (L, L), 1)
    ones_col = jnp.where(lax.broadcasted_iota(jnp.int32, (L, LANE), 1) == 0, 1.0, 0.0).astype(CDT)

    for d, (qk_ref, v_ref, g_ref, h_ref) in enumerate(((qkf_ref, vf_ref, gf_ref, hf_ref),
                                                         (qkb_ref, vb_ref, gb_ref, hb_ref))):
        rev = d == 1
        qk_all = jnp.where(first, qkc_ref[0], qk_ref[0])
        v_all = jnp.where(first, vc_ref[0], v_ref[0])
        g = jnp.where(first, gc_ref[0], g_ref[0]) + bif_ref[...]
        li = g[:, :LANE]
        lf = jax.nn.log_sigmoid(g[:, LANE:])
        bc = _scan_rows(lf, jnp.add, 0.0, rev)
        r = li - bc
        cm = _scan_rows(r, jnp.maximum, NEG, rev)
        m_row = m_ref[d, 0:1, :]
        big_m = jnp.maximum(m_row, cm)
        a_all = jnp.exp(m_row - big_m)
        floor_all = jnp.exp(-(bc + big_m))
        m_fin = jnp.maximum(m_row, jnp.max(r, axis=0, keepdims=True))
        wk_all = jnp.exp(r - m_fin)
        decay_all = jnp.exp(m_row - m_fin)
        b_last = bc[0:1, :] if rev else bc[L - 1:L, :]
        m_ref[d, 0:1, :] = b_last + m_fin
        r_t = r.T
        causal = (col >= row) if rev else (col <= row)

        for h in range(HEADS):
            c = d * HEADS + h
            q = qk_all[:, h * LANE:(h + 1) * LANE]
            k = qk_all[:, (HEADS + h) * LANE:(HEADS + h + 1) * LANE]
            v_ext = jnp.concatenate([v_all[:, h * LANE:(h + 1) * LANE], ones_col], axis=1)
            dm = jnp.exp(jnp.where(causal, r_t[c:c + 1, :] - big_m[:, c:c + 1], NEG))
            s = (_qk(q, k) * dm).astype(CDT)
            c_old = c_ref[c]
            nd = (a_all[:, c:c + 1] * jnp.dot(q, c_old.astype(CDT), preferred_element_type=F32)
                  + jnp.dot(s, v_ext, preferred_element_type=F32))
            den = jnp.maximum(jnp.abs(nd[:, LANE:LANE + 1]), floor_all[:, c:c + 1])
            h_ref[0, :, h * LANE:(h + 1) * LANE] = nd[:, :LANE] / den
            kw_t = (k.astype(F32) * wk_all[:, c:c + 1]).T.astype(CDT)
            upd = jnp.dot(kw_t, v_ext, preferred_element_type=F32)
            c_ref[c] = decay_all[:, c:c + 1] * c_old + upd


def _mlstm_scan(qk_l, p1, g_l, qk_c, p1c, g_c, b_if2):
    L = ML_CHUNK
    bsz, n, _ = qk_l.shape
    nc = qk_c.shape[1]
    nl = n // L
    steps = nl + 1
    vblk = BLK_D_V * LANE // BR_W
    fwd = lambda j: jnp.maximum(j - 1, 0)
    bwd = lambda j: jnp.minimum(nl - j, nl - 1)
    return pl.pallas_call(
        _mlstm_kernel,
        grid=(bsz, steps),
        in_specs=[pl.BlockSpec((1, L, 2 * BR_W), lambda b, j: (b, 0, 0)),
                  pl.BlockSpec((1, L, BR_W), lambda b, j: (b, 0, vblk)),
                  pl.BlockSpec((1, L, 2 * LANE), lambda b, j: (b, 0, 0)),
                  pl.BlockSpec((1, L, 2 * BR_W), lambda b, j: (b, fwd(j), 0)),
                  pl.BlockSpec((1, L, BR_W), lambda b, j: (b, fwd(j), vblk)),
                  pl.BlockSpec((1, L, 2 * LANE), lambda b, j: (b, fwd(j), 0)),
                  pl.BlockSpec((1, L, 2 * BR_W), lambda b, j: (b, bwd(j), 0)),
                  pl.BlockSpec((1, L, BR_W), lambda b, j: (b, bwd(j), vblk)),
                  pl.BlockSpec((1, L, 2 * LANE), lambda b, j: (b, bwd(j), 0)),
                  pl.BlockSpec((1, 2 * LANE), lambda b, j: (0, 0))],
        out_specs=[pl.BlockSpec((1, L, BR_W), lambda b, j: (b, j, 0)),
                   pl.BlockSpec((1, L, BR_W), lambda b, j: (b, steps - 1 - j, 0))],
        out_shape=[jax.ShapeDtypeStruct((bsz, n + nc, BR_W), F32),
                   jax.ShapeDtypeStruct((bsz, n + nc, BR_W), F32)],
        scratch_shapes=[pltpu.VMEM((2 * HEADS, LANE, 2 * LANE), F32),
                        pltpu.VMEM((2, 8, LANE), F32)],
        compiler_params=_cparams(("parallel", "arbitrary")),
        name="mlstm_scan",
    )(qk_c, p1c, g_c, qk_l, p1, g_l, qk_l, p1, g_l, b_if2)


def _mlstm_out_kernel(hf_ref, hb_ref, o_ref, g_ref, gn_ref, y_ref):
    for h in range(HEADS):
        sl = slice(h * LANE, (h + 1) * LANE)
        hh = hf_ref[0, :, sl] + hb_ref[0, :, sl]
        hn = hh * lax.rsqrt(jnp.mean(hh * hh, axis=-1, keepdims=True) + EPS)
        y = hn * gn_ref[:, sl] * jax.nn.sigmoid(o_ref[0, :, sl].astype(F32))
        y_ref[0, :, sl] = (y * _silu(g_ref[0, :, sl].astype(F32))).astype(y_ref.dtype)


def _mlstm_output(h_f, h_b, p1, gnorm, f_off, b_off):
    bsz, n, _ = p1.shape
    L = ML_CHUNK
    wblk = BR_W // LANE
    return pl.pallas_call(
        _mlstm_out_kernel,
        grid=(bsz, n // L),
        in_specs=[pl.BlockSpec((1, L, BR_W), lambda b, i: (b, f_off + i, 0)),
                  pl.BlockSpec((1, L, BR_W), lambda b, i: (b, b_off + i, 0)),
                  pl.BlockSpec((1, L, BR_W), lambda b, i: (b, i, BLK_D_O // wblk)),
                  pl.BlockSpec((1, L, BR_W), lambda b, i: (b, i, BLK_D_GATE // wblk)),
                  pl.BlockSpec((1, BR_W), lambda b, i: (0, 0))],
        out_specs=pl.BlockSpec((1, L, BR_W), lambda b, i: (b, i, 0)),
        out_shape=jax.ShapeDtypeStruct((bsz, n, BR_W), CDT),
        compiler_params=_cparams(("parallel", "parallel")),
        name="mlstm_output",
    )(h_f, h_b, p1, p1, gnorm.reshape(1, BR_W))


def _merge_kernel(x_ref, mod_ref, ya, yb, yc, yd, mp_ref, wbr_ref, wout_ref, gfin_ref, o_ref, *, d, final_norm):
    acc = None
    for i, y_ref in enumerate((ya, yb, yc, yd)):
        z = jnp.dot(y_ref[...], wbr_ref[i], preferred_element_type=F32)
        t = jax.nn.sigmoid(mp_ref[:, i * d:(i + 1) * d].astype(F32)) * z
        acc = t if acc is None else acc + t
    out = jnp.dot(acc.astype(CDT), wout_ref[...], preferred_element_type=F32)
    x = x_ref[...] + mod_ref[0, 2:3, :] * out
    if final_norm:
        x = x * lax.rsqrt(jnp.mean(x * x, axis=-1, keepdims=True) + EPS) * gfin_ref[...]
    o_ref[...] = x


def _merge(x2, mod, mod_row, ys, mp, w_br, w_out, g_final, final_norm, tm):
    t, d = x2.shape
    y_spec = pl.BlockSpec((tm, BR_W), lambda i: (i, 0))
    return pl.pallas_call(
        functools.partial(_merge_kernel, d=d, final_norm=final_norm),
        grid=(t // tm,),
        in_specs=[pl.BlockSpec((tm, d), lambda i: (i, 0)),
                  pl.BlockSpec((1, 3, d), lambda i: (mod_row(i), 0, 0)),
                  y_spec, y_spec, y_spec, y_spec,
                  pl.BlockSpec((tm, N_BRANCH * d), lambda i: (i, 0)),
                  pl.BlockSpec((N_BRANCH, BR_W, d), lambda i: (0, 0, 0), pipeline_mode=pl.Buffered(1)),
                  pl.BlockSpec((d, d), lambda i: (0, 0), pipeline_mode=pl.Buffered(1)),
                  pl.BlockSpec((1, d), lambda i: (0, 0))],
        out_specs=pl.BlockSpec((tm, d), lambda i: (i, 0)),
        out_shape=jax.ShapeDtypeStruct((t, d), F32),
        compiler_params=_cparams(("parallel",)),
        name="merge",
    )(x2, mod, *ys, mp, w_br, w_out, g_final.reshape(1, d))


def _split_offsets(d):
    widths = (3 * BR_W, BR_W, BR_W, BR_W, MLA_Q_LORA, MLA_KV_LORA, MLA_ROPE, BR_W,
              3 * BR_W, BR_W, 4 * HEADS, BR_W, N_BRANCH * d)
    names = ('a_qkv', 'a_gate', 'b_in', 'b_gate', 'c_q', 'c_kv', 'c_kr', 'c_gate',
             'd_qkv', 'd_o', 'd_if', 'd_gate', 'merge')
    offs = np.concatenate([[0], np.cumsum(widths)])
    return {nm: (int(offs[i]), int(offs[i + 1])) for i, nm in enumerate(names)}


def _rope_swap_perm():
    half = MLA_ROPE // 2
    q = half // 2
    return np.concatenate([np.arange(q, half), np.arange(0, q), np.arange(half + q, MLA_ROPE), np.arange(half, half + q)])


def _prep_w_in(w_in):
    nl, d, _ = w_in.shape
    o = _split_offsets(d)
    w_in = w_in.astype(CDT)
    cut = lambda nm: w_in[:, :, o[nm][0]:o[nm][1]]
    kr = cut('c_kr')
    d_if = cut('d_if')
    d_if = d_if.reshape(nl, d, 2, 2, HEADS)
    zpad = jnp.zeros((nl, d, LANE - 2 * HEADS), w_in.dtype)
    li = jnp.concatenate([d_if[:, :, :, 0, :].reshape(nl, d, 2 * HEADS), zpad], axis=-1)
    lf = jnp.concatenate([d_if[:, :, :, 1, :].reshape(nl, d, 2 * HEADS), zpad], axis=-1)
    w1 = jnp.concatenate([
        w_in[:, :, :o['c_kv'][1]], cut('c_gate'), cut('d_qkv'), cut('d_o'), cut('d_gate'),
        kr, kr[:, :, _rope_swap_perm()], jnp.zeros((nl, d, (BLK_D_LI - BLK_C_KR - 1) * LANE), w_in.dtype),
        li, lf], axis=-1)
    assert w1.shape[-1] == N1, w1.shape
    return w1.astype(CDT), cut('merge').astype(CDT)


def _prep_b_if(b_if):
    nl = b_if.shape[0]
    b = b_if.reshape(nl, 2, 2, HEADS)
    zpad = jnp.zeros((nl, LANE - 2 * HEADS), b_if.dtype)
    li = jnp.concatenate([b[:, :, 0, :].reshape(nl, 2 * HEADS), zpad], axis=-1)
    lf = jnp.concatenate([b[:, :, 1, :].reshape(nl, 2 * HEADS), zpad], axis=-1)
    return jnp.concatenate([li, lf], axis=-1).reshape(nl, 1, 2 * LANE)


def _prep_w_uq(w_uq):
    nl, r, _ = w_uq.shape
    w = w_uq.reshape(nl, r, HEADS, MLA_NOPE + MLA_ROPE)
    rope = w[..., MLA_NOPE:]
    w = jnp.concatenate([w[..., :MLA_NOPE], rope, rope[..., _rope_swap_perm()]], axis=-1)
    return w.reshape(nl, r, HEADS * 2 * LANE).astype(CDT)


def kernel(x, c, ctx, c_ctx, norm_g, w_mod, b_mod, w_in, na_rpb, pool_w, pool_scale, mla_gq, mla_gkv, w_uq, w_ukv,
           conv_w, b_if, ml_gnorm, w_br, w_out, g_final):
    bsz, n, d = x.shape
    nc = ctx.shape[1]
    depth = w_in.shape[0]
    rows = n // GRID_W
    assert n % (NA_QROWS * GRID_W) == 0 and rows >= NA_KROWS
    assert nc == ML_CHUNK and n % ML_CHUNK == 0 and nc % 8 == 0

    w1_all, w2_all = _prep_w_in(w_in)
    wq_all = _prep_w_uq(w_uq)
    wkv_all = w_ukv.astype(CDT)
    wbr_all = w_br.astype(CDT)
    wout_all = w_out.astype(CDT)
    poolw_all = pool_w.astype(CDT)
    bif_all = _prep_b_if(b_if)
    bias_all = _na_column_tables(na_rpb)
    tab_lat = _rope_tables(n, False)
    tab_ctx = _rope_tables(nc, True)

    n_cond = -(-(bsz + 1) // 8) * 8
    cond = jnp.concatenate([c, c_ctx[None, :], jnp.zeros((n_cond - bsz - 1, d), F32)], axis=0)
    mods = _modulation(cond, w_mod, b_mod)
    mods = mods.reshape(depth, n_cond, 3, d)

    tm_l = _pick(n, 1024)
    tm_c = _pick(bsz * nc, 1024)
    tn1 = _pick(N1, 1536) if N1 % 1536 == 0 else _pick(N1, 512)
    tn2 = _pick(N_BRANCH * d, 1024)
    tm_merge = _pick(n, 256)
    lat_row = lambda i: i // (n // tm_l)
    ctx_row = lambda i: bsz
    lat_row_m = lambda i: i // (n // tm_merge)

    xl = x.reshape(bsz * n, d)
    xc = ctx.reshape(bsz * nc, d)
    for l in range(depth):
        need_ctx = l < depth - 1
        last = l == depth - 1
        mod = mods[l]
        p1, gates = _inproj(xl, mod, lat_row, norm_g[l], w1_all[l], tm_l, tn1, 2 * LANE)
        p2 = _inproj(xl, mod, lat_row, norm_g[l], w2_all[l], tm_l, tn2, 0)[0]
        p1c, gates_c = _inproj(xc, mod, ctx_row, norm_g[l], w1_all[l], tm_c, tn1, 2 * LANE)
        p1_3 = p1.reshape(bsz, n, N1)
        p1c_3 = p1c.reshape(bsz, nc, N1)

        ya = _neighbourhood(p1_3, p1c_3, bias_all[l], bsz, n, nc)
        yb = _pool(p1_3, poolw_all[l], pool_scale[l])
        q_l, k_l, v_l = _mla_proj(p1, n, mla_gq[l], mla_gkv[l], wq_all[l], wkv_all[l], tab_lat)
        q_c, k_c, v_c = _mla_proj(p1c, nc, mla_gq[l], mla_gkv[l], wq_all[l], wkv_all[l], tab_ctx)
        r3 = lambda a, m: a.reshape(bsz, m, a.shape[-1])
        yc = _mla_attention(r3(q_l, n), r3(k_l, n), r3(v_l, n), r3(k_c, nc), r3(v_c, nc), p1_3, bsz, n, nc)
        qk_l = _short_conv(p1_3, conv_w[l])
        qk_c = _short_conv(p1c_3, conv_w[l])
        g_l = gates.reshape(bsz, n, 2 * LANE)
        g_c = gates_c.reshape(bsz, nc, 2 * LANE)
        h_f, h_b = _mlstm_scan(qk_l, p1_3, g_l, qk_c, p1c_3, g_c, bif_all[l])
        yd = _mlstm_output(h_f, h_b, p1_3, ml_gnorm[l], nc // ML_CHUNK, 0)

        f2 = lambda a: a.reshape(-1, a.shape[-1])
        xl_new = _merge(xl, mod, lat_row_m, (f2(ya), f2(yb), f2(yc), f2(yd)), p2, wbr_all[l], wout_all[l],
                        g_final, last, tm_merge)
        if need_ctx:
            p2c = _inproj(xc, mod, ctx_row, norm_g[l], w2_all[l], tm_c, tn2, 0)[0]
            ya_c = _ctx_attention(p1c_3, BLK_A_Q, LANE, p1c_3, BLK_A_K, LANE, p1c_3, BLK_A_V, 1, p1c_3, BLK_A_GATE,
                                  HEAD_DIM ** -0.5, False)
            yb_c = _pool(p1c_3, poolw_all[l], pool_scale[l])
            yc_c = _ctx_attention(r3(q_c, nc), 0, 2 * LANE, r3(k_c, nc), 0, 2 * LANE, r3(v_c, nc), 0, 2,
                                  p1c_3, BLK_C_GATE, 1.0, True)
            yd_c = _mlstm_output(h_f, h_b, p1c_3, ml_gnorm[l], 0, n // ML_CHUNK)
            xc = _merge(xc, mod, ctx_row, (f2(ya_c), f2(yb_c), f2(yc_c), f2(yd_c)), p2c, wbr_all[l], wout_all[l],
                        g_final, False, _pick(nc * bsz, 256))
        xl = xl_new
    return xl.reshape(bsz, n, d)
```

```python
import functools

import numpy as np
import jax
import jax.numpy as jnp
from jax import lax
from jax.experimental import pallas as pl
from jax.experimental.pallas import tpu as pltpu

F32 = jnp.float32
CDT = jnp.bfloat16

GRID_W = 64
EPS = 1e-6
N_BRANCH = 4
BR_W = 512
HEADS = 4
HEAD_DIM = BR_W // HEADS
NA_ROWS = 8
NA_COLS = 16
POOL_WINDOWS = (2, 4, 8, 16)
MLA_NOPE = 128
MLA_ROPE = 64
MLA_Q_LORA = 512
MLA_KV_LORA = 512
ROPE_THETA = 10000.0
CONV_K = 4

LANE = 128
VMEM_LIMIT = 56 * 1024 * 1024
NEG = -1e30

BLK_A_Q, BLK_A_K, BLK_A_V, BLK_A_GATE = 0, 4, 8, 12
BLK_B_IN, BLK_B_GATE = 16, 20
BLK_C_Q, BLK_C_KV, BLK_C_GATE = 24, 28, 32
BLK_D_Q, BLK_D_K, BLK_D_V, BLK_D_O, BLK_D_GATE = 36, 40, 44, 48, 52
BLK_C_KR = 56
BLK_D_LI, BLK_D_LF = 58, 59
N1_BLOCKS = 60
N1 = N1_BLOCKS * LANE

NA_QROWS = 8
NA_KROWS = 16
ML_CHUNK = 256
MLA_ROW_SPLIT = 2


def _cparams(sem):
    return pltpu.CompilerParams(dimension_semantics=sem, vmem_limit_bytes=VMEM_LIMIT)


def _silu(x):
    return x * jax.nn.sigmoid(x)


def _pick(n, pref):
    t = min(n, pref)
    while n % t:
        t //= 2
    return t


def _mod_kernel(c_ref, w_ref, b_ref, o_ref):
    c = _silu(c_ref[...]).astype(CDT)
    o_ref[0] = jnp.dot(c, w_ref[0].astype(CDT), preferred_element_type=F32) + b_ref[0]


def _modulation(cond, w_mod, b_mod):
    nl, d, d3 = w_mod.shape
    r = cond.shape[0]
    tn = _pick(d3, 1024)
    return pl.pallas_call(
        _mod_kernel,
        grid=(nl, d3 // tn),
        in_specs=[pl.BlockSpec((r, d), lambda l, j: (0, 0)),
                  pl.BlockSpec((1, d, tn), lambda l, j: (l, 0, j)),
                  pl.BlockSpec((1, 1, tn), lambda l, j: (l, 0, j))],
        out_specs=pl.BlockSpec((1, r, tn), lambda l, j: (l, 0, j)),
        out_shape=jax.ShapeDtypeStruct((nl, r, d3), F32),
        compiler_params=_cparams(("parallel", "parallel")),
        name="modulation",
    )(cond, w_mod, b_mod.reshape(nl, 1, d3))


def _inproj_kernel(x_ref, mod_ref, g_ref, w_ref, o_ref, *rest, n_f32_cols):
    hn_ref = rest[-1]
    j = pl.program_id(1)

    @pl.when(j == 0)
    def _():
        x = x_ref[...]
        y = x * lax.rsqrt(jnp.mean(x * x, axis=-1, keepdims=True) + EPS) * g_ref[...]
        hn = y * (1.0 + mod_ref[0, 1:2, :]) + mod_ref[0, 0:1, :]
        hn_ref[...] = hn.astype(CDT)

    acc = jnp.dot(hn_ref[...], w_ref[...], preferred_element_type=F32)
    o_ref[...] = acc.astype(o_ref.dtype)
    if n_f32_cols:
        f_ref = rest[0]

        @pl.when(j == pl.num_programs(1) - 1)
        def _():
            f_ref[...] = acc[:, acc.shape[1] - n_f32_cols:]


def _inproj(x2, mod, mod_row, norm_g, w, tm, tn, n_f32_cols):
    t, d = x2.shape
    n = w.shape[1]
    out_shape = [jax.ShapeDtypeStruct((t, n), CDT)]
    out_specs = [pl.BlockSpec((tm, tn), lambda i, j: (i, j))]
    if n_f32_cols:
        out_shape.append(jax.ShapeDtypeStruct((t, n_f32_cols), F32))
        out_specs.append(pl.BlockSpec((tm, n_f32_cols), lambda i, j: (i, 0)))
    res = pl.pallas_call(
        functools.partial(_inproj_kernel, n_f32_cols=n_f32_cols),
        grid=(t // tm, n // tn),
        in_specs=[pl.BlockSpec((tm, d), lambda i, j: (i, 0)),
                  pl.BlockSpec((1, 3, d), lambda i, j: (mod_row(i), 0, 0)),
                  pl.BlockSpec((1, d), lambda i, j: (0, 0)),
                  pl.BlockSpec((d, tn), lambda i, j: (0, j))],
        out_specs=out_specs,
        out_shape=out_shape,
        scratch_shapes=[pltpu.VMEM((tm, d), CDT)],
        compiler_params=_cparams(("parallel", "arbitrary")),
        name="inproj",
    )(x2, mod, norm_g.reshape(1, d), w)
    return res


def _softmax_pv(parts, exp=jnp.exp):
    m = None
    for s, _ in parts:
        mi = jnp.max(s, axis=-1, keepdims=True)
        m = mi if m is None else jnp.maximum(m, mi)
    l = None
    o = None
    for s, v in parts:
        p = exp(s - m)
        li = jnp.sum(p, axis=-1, keepdims=True)
        oi = jnp.dot(p.astype(CDT), v, preferred_element_type=F32)
        l = li if l is None else l + li
        o = oi if o is None else o + oi
    return o / l


def _qk(q, k):
    return lax.dot_general(q, k, (((1,), (1,)), ((), ())), preferred_element_type=F32)


def _na_row_offsets(rows, jt):
    start = int(np.clip(jt * NA_QROWS - NA_ROWS // 2, 0, rows - NA_KROWS))
    r = jt * NA_QROWS + np.arange(NA_QROWS)
    keyrow = start + np.arange(NA_KROWS)
    rs = np.clip(r - NA_ROWS // 2, 0, rows - NA_ROWS)
    row_ok = (keyrow[None, :] >= rs[:, None]) & (keyrow[None, :] < rs[:, None] + NA_ROWS)
    return np.where(row_ok, keyrow[None, :] - r[:, None] + NA_ROWS - 1, -1)


def _na_kernel(q_ref, k_ref, v_ref, kc_ref, vc_ref, g_ref, bc_ref, o_ref, bias_ref, *, rows, scale):
    j = pl.program_id(2)
    nj = rows // NA_QROWS
    w = GRID_W
    left = lax.broadcasted_iota(jnp.int32, (w, LANE), 1) < w

    def build(jt):
        roff = _na_row_offsets(rows, jt)
        neg = jnp.full((w, LANE), NEG, F32)
        for g in range(NA_QROWS):
            for kp in range(NA_KROWS // 2):
                ra, rb = int(roff[g, 2 * kp]), int(roff[g, 2 * kp + 1])
                ta = bc_ref[0, ra] if ra >= 0 else neg
                tb = bc_ref[0, rb] if rb >= 0 else neg
                tile = neg if (ra < 0 and rb < 0) else jnp.where(left, ta, tb)
                bias_ref[g * w:(g + 1) * w, kp * LANE:(kp + 1) * LANE] = tile

    pl.when(j == 0)(lambda: build(0))
    if nj > 2:
        pl.when(j == 1)(lambda: build(1))
    pl.when(j == nj - 1)(lambda: build(nj - 1))

    start = jnp.clip(j * NA_QROWS - NA_ROWS // 2, 0, rows - NA_KROWS) * GRID_W
    start = pl.multiple_of(start, GRID_W)
    kw = k_ref[0, pl.ds(start, NA_KROWS * GRID_W), :]
    vw = v_ref[0, pl.ds(start, NA_KROWS * GRID_W), :]
    half = q_ref.shape[1] // 2
    for rs in (slice(0, half), slice(half, 2 * half)):
        q = q_ref[0, rs, :]
        s_loc = _qk(q, kw) * scale + bias_ref[rs, :]
        s_ctx = _qk(q, kc_ref[0]) * scale
        o = _softmax_pv([(s_loc, vw), (s_ctx, vc_ref[0])])
        o_ref[0, rs, :] = (o * _silu(g_ref[0, rs, :].astype(F32))).astype(o_ref.dtype)


def _na_column_tables(rpb):
    w = GRID_W
    nl, nh, n_ro, n_co = rpb.shape
    cq = np.arange(w)
    cstart = np.clip(cq - NA_COLS // 2, 0, w - NA_COLS)
    col_ok = (cq[None, :] >= cstart[:, None]) & (cq[None, :] < cstart[:, None] + NA_COLS)
    period = 2 * w + 1
    v = jnp.concatenate([rpb[..., NA_COLS - 1:], jnp.zeros((nl, nh, n_ro, period - n_co), rpb.dtype),
                         rpb[..., :NA_COLS - 1]], axis=-1).astype(F32)
    t = jnp.tile(v, (1, 1, 1, w))[..., :w * 2 * w].reshape(nl, nh, n_ro, w, 2 * w)[..., :w]
    t = jnp.where(jnp.asarray(col_ok), t, NEG)
    return jnp.concatenate([t, t], axis=-1)


def _neighbourhood(p1, p1c, bc, bsz, n, nc):
    rows = n // GRID_W
    nj = rows // NA_QROWS
    tq = NA_QROWS * GRID_W
    n_ro = bc.shape[1]
    return pl.pallas_call(
        functools.partial(_na_kernel, rows=rows, scale=HEAD_DIM ** -0.5),
        grid=(bsz, HEADS, nj),
        in_specs=[pl.BlockSpec((1, tq, LANE), lambda b, h, j: (b, j, BLK_A_Q + h)),
                  pl.BlockSpec((1, n, LANE), lambda b, h, j: (b, 0, BLK_A_K + h)),
                  pl.BlockSpec((1, n, LANE), lambda b, h, j: (b, 0, BLK_A_V + h)),
                  pl.BlockSpec((1, nc, LANE), lambda b, h, j: (b, 0, BLK_A_K + h)),
                  pl.BlockSpec((1, nc, LANE), lambda b, h, j: (b, 0, BLK_A_V + h)),
                  pl.BlockSpec((1, tq, LANE), lambda b, h, j: (b, j, BLK_A_GATE + h)),
                  pl.BlockSpec((1, n_ro, GRID_W, 2 * GRID_W), lambda b, h, j: (h, 0, 0, 0))],
        out_specs=pl.BlockSpec((1, tq, LANE), lambda b, h, j: (b, j, h)),
        out_shape=jax.ShapeDtypeStruct((bsz, n, BR_W), CDT),
        scratch_shapes=[pltpu.VMEM((tq, NA_KROWS * GRID_W), F32)],
        compiler_params=_cparams(("parallel", "parallel", "arbitrary")),
        name="neighbourhood_attention",
    )(p1, p1, p1, p1c, p1c, p1, bc)


def _ctx_attn_kernel(q_ref, k_ref, v_ref, g_ref, o_ref, *, scale, base2):
    s = _qk(q_ref[0], k_ref[0]) * scale
    o = _softmax_pv([(s, v_ref[0])], jnp.exp2 if base2 else jnp.exp)
    o_ref[0] = (o * _silu(g_ref[0].astype(F32))).astype(o_ref.dtype)


def _ctx_attention(q, qblk, qw, k, kblk, kw, v, vblk, vstep, gate, gblk, scale, base2):
    bsz, nc, _ = q.shape
    return pl.pallas_call(
        functools.partial(_ctx_attn_kernel, scale=scale, base2=base2),
        grid=(bsz, HEADS),
        in_specs=[pl.BlockSpec((1, nc, qw), lambda b, h: (b, 0, qblk + h)),
                  pl.BlockSpec((1, nc, kw), lambda b, h: (b, 0, kblk + h)),
                  pl.BlockSpec((1, nc, LANE), lambda b, h: (b, 0, vblk + vstep * h)),
                  pl.BlockSpec((1, nc, LANE), lambda b, h: (b, 0, gblk + h))],
        out_specs=pl.BlockSpec((1, nc, LANE), lambda b, h: (b, 0, h)),
        out_shape=jax.ShapeDtypeStruct((bsz, nc, BR_W), CDT),
        compiler_params=_cparams(("parallel", "parallel")),
        name="context_attention",
    )(q, k, v, gate)


POOL_PAD = 16
POOL_HALO = 8


def _pool_kernel(u_ref, g_ref, w_ref, sc_ref, o_ref, buf_a, buf_b, *, n):
    gi = pl.program_id(1)
    x = u_ref[0].astype(F32)
    t = lax.broadcasted_iota(jnp.int32, (n, LANE), 0)
    ext = n + 2 * POOL_HALO

    def finish(wsum, w):
        lo = jnp.maximum(t - w // 2, 0)
        hi = jnp.minimum(t + (w - 1 - w // 2), n - 1)
        cnt = (hi - lo + 1).astype(F32)
        d = wsum / cnt - x
        y = jnp.dot(d.astype(CDT), w_ref[0], preferred_element_type=F32) * sc_ref[...]
        o_ref[0] = (y * _silu(g_ref[0].astype(F32))).astype(o_ref.dtype)

    def window(ref, off):
        return ref[pl.ds(POOL_PAD + off, n), :]

    def wide(ref, off):
        return ref[pl.ds(POOL_PAD - POOL_HALO + off, ext), :]

    for buf in (buf_a, buf_b):
        buf[pl.ds(0, POOL_PAD), :] = jnp.zeros((POOL_PAD, LANE), F32)
        buf[pl.ds(POOL_PAD + n, POOL_PAD), :] = jnp.zeros((POOL_PAD, LANE), F32)
    buf_a[pl.ds(POOL_PAD, n), :] = x

    for level, w in enumerate(POOL_WINDOWS):
        @pl.when(gi == level)
        def _(level=level, w=w):
            if level == 0:
                finish(window(buf_a, -1) + window(buf_a, 0), w)
                return
            buf_b[pl.ds(POOL_PAD - POOL_HALO, ext), :] = wide(buf_a, -1) + wide(buf_a, 0)
            src, dst = buf_b, buf_a
            for lv in range(1, level):
                h = 1 << (lv - 1)
                dst[pl.ds(POOL_PAD - POOL_HALO, ext), :] = wide(src, -h) + wide(src, h)
                src, dst = dst, src
            h = 1 << (level - 1)
            finish(window(src, -h) + window(src, h), w)


def _pool(p1, pool_w, pool_scale):
    bsz, n, _ = p1.shape
    ng = len(POOL_WINDOWS)
    return pl.pallas_call(
        functools.partial(_pool_kernel, n=n),
        grid=(bsz, ng),
        in_specs=[pl.BlockSpec((1, n, LANE), lambda b, g: (b, 0, BLK_B_IN + g)),
                  pl.BlockSpec((1, n, LANE), lambda b, g: (b, 0, BLK_B_GATE + g)),
                  pl.BlockSpec((1, LANE, LANE), lambda b, g: (g, 0, 0)),
                  pl.BlockSpec((1, LANE), lambda b, g: (0, g))],
        out_specs=pl.BlockSpec((1, n, LANE), lambda b, g: (b, 0, g)),
        out_shape=jax.ShapeDtypeStruct((bsz, n, BR_W), CDT),
        scratch_shapes=[pltpu.VMEM((n + 2 * POOL_PAD, LANE), F32),
                        pltpu.VMEM((n + 2 * POOL_PAD, LANE), F32)],
        compiler_params=_cparams(("parallel", "arbitrary")),
        name="pool_branch",
    )(p1, p1, pool_w, pool_scale.reshape(1, BR_W))


def _rms(x, g):
    return x * lax.rsqrt(jnp.mean(x * x, axis=-1, keepdims=True) + EPS) * g


def _mla_proj_kernel(cq_ref, ckv_ref, kr_ref, gq_ref, gkv_ref, wq_ref, wkv_ref,
                     tq_c, tq_s, tk_c, tk_s, q_ref, k_ref, v_ref, *, scale):
    hq = _rms(cq_ref[...].astype(F32), gq_ref[...]).astype(CDT)
    hkv = _rms(ckv_ref[...].astype(F32), gkv_ref[...]).astype(CDT)
    q = jnp.dot(hq, wq_ref[...], preferred_element_type=F32) * scale
    kv = jnp.dot(hkv, wkv_ref[...], preferred_element_type=F32)
    kr = kr_ref[...].astype(F32)
    k_rot = (kr * tk_c[...] + pltpu.roll(kr, LANE // 2, 1) * tk_s[...]).astype(CDT)
    ones_col = jnp.where(lax.broadcasted_iota(jnp.int32, kr.shape, 1) == 0, 1.0, 0.0).astype(CDT)
    for h in range(HEADS):
        lo = 2 * h * LANE
        q_rope = q[:, lo + LANE:lo + 2 * LANE]
        q_rot = q_rope * tq_c[...] + pltpu.roll(q_rope, LANE // 2, 1) * tq_s[...]
        q_ref[:, lo:lo + LANE] = q[:, lo:lo + LANE].astype(CDT)
        q_ref[:, lo + LANE:lo + 2 * LANE] = q_rot.astype(CDT)
        k_ref[:, lo:lo + LANE] = kv[:, lo:lo + LANE].astype(CDT)
        k_ref[:, lo + LANE:lo + 2 * LANE] = k_rot
        v_ref[:, lo:lo + LANE] = kv[:, lo + LANE:lo + 2 * LANE].astype(CDT)
        v_ref[:, lo + LANE:lo + 2 * LANE] = ones_col


def _rope_tables(n, is_ctx):
    half = MLA_ROPE // 2
    nfreq = half // 2
    inv = ROPE_THETA ** (-np.arange(nfreq, dtype=np.float64) / nfreq)
    pos = np.arange(n)
    ang_r = (pos // GRID_W)[:, None] * inv
    ang_c = (pos % GRID_W)[:, None] * inv
    cos = np.concatenate([np.cos(ang_r), np.cos(ang_r), np.cos(ang_c), np.cos(ang_c)], axis=1)
    sin = np.concatenate([-np.sin(ang_r), np.sin(ang_r), -np.sin(ang_c), np.sin(ang_c)], axis=1)
    zero = np.zeros_like(cos)
    one = np.ones_like(cos)
    tq_c = np.concatenate([cos, zero], axis=1)
    tq_s = np.concatenate([sin, one], axis=1)
    if is_ctx:
        tk_c = np.concatenate([zero, zero], axis=1)
        tk_s = np.concatenate([zero, one], axis=1)
    else:
        tk_c = np.concatenate([cos, zero], axis=1)
        tk_s = np.concatenate([sin, zero], axis=1)
    return tuple(jnp.asarray(a, F32) for a in (tq_c, tq_s, tk_c, tk_s))


def _mla_proj(p1f, n, gq, gkv, wq, wkv, tables):
    t = p1f.shape[0]
    tm = _pick(n, 512)
    per_seq = n // tm
    wide = HEADS * 2 * LANE
    tab_spec = pl.BlockSpec((tm, LANE), lambda i: (i % per_seq, 0))
    return pl.pallas_call(
        functools.partial(_mla_proj_kernel, scale=(MLA_NOPE + MLA_ROPE) ** -0.5 * np.log2(np.e)),
        grid=(t // tm,),
        in_specs=[pl.BlockSpec((tm, MLA_Q_LORA), lambda i: (i, BLK_C_Q * LANE // MLA_Q_LORA)),
                  pl.BlockSpec((tm, MLA_KV_LORA), lambda i: (i, BLK_C_KV * LANE // MLA_KV_LORA)),
                  pl.BlockSpec((tm, LANE), lambda i: (i, BLK_C_KR)),
                  pl.BlockSpec((1, MLA_Q_LORA), lambda i: (0, 0)),
                  pl.BlockSpec((1, MLA_KV_LORA), lambda i: (0, 0)),
                  pl.BlockSpec((MLA_Q_LORA, wide), lambda i: (0, 0)),
                  pl.BlockSpec((MLA_KV_LORA, wide), lambda i: (0, 0)),
                  tab_spec, tab_spec, tab_spec, tab_spec],
        out_specs=[pl.BlockSpec((tm, wide), lambda i: (i, 0)),
                   pl.BlockSpec((tm, wide), lambda i: (i, 0)),
                   pl.BlockSpec((tm, wide), lambda i: (i, 0))],
        out_shape=[jax.ShapeDtypeStruct((t, wide), CDT),
                   jax.ShapeDtypeStruct((t, wide), CDT),
                   jax.ShapeDtypeStruct((t, wide), CDT)],
        compiler_params=_cparams(("parallel",)),
        name="mla_projection",
    )(p1f, p1f, p1f, gq.reshape(1, -1), gkv.reshape(1, -1), wq, wkv, *tables)


def _mla_attn_kernel(q_ref, k_ref, v_ref, kc_ref, vc_ref, g_ref, o_ref, s_ref, *, n, nc, tk):
    tq = q_ref.shape[1]
    chunks = [(kc_ref, vc_ref, 0, nc, 0)] + [(k_ref, v_ref, c * tk, tk, nc + c * tk) for c in range(n // tk)]

    for rows in (slice(r, r + tq // MLA_ROW_SPLIT) for r in range(0, tq, tq // MLA_ROW_SPLIT)):
        q = q_ref[0, rows, :]
        mv = jnp.full((q.shape[0], LANE), NEG, F32)
        for kr, _, start, size, col in chunks:
            s = _qk(q, kr[0, pl.ds(start, size), :])
            s_ref[rows, col:col + size] = s
            for i in range(size // LANE):
                mv = jnp.maximum(mv, s[:, i * LANE:(i + 1) * LANE])
        m = jnp.max(mv, axis=-1, keepdims=True)

        acc = jnp.zeros((q.shape[0], 2 * LANE), F32)
        for _, vr, start, size, col in chunks:
            p = jnp.exp2(s_ref[rows, col:col + size] - m).astype(CDT)
            acc = acc + jnp.dot(p, vr[0, pl.ds(start, size), :], preferred_element_type=F32)
        o = acc[:, :LANE] / acc[:, LANE:LANE + 1]
        o_ref[0, rows, :] = (o * _silu(g_ref[0, rows, :].astype(F32))).astype(o_ref.dtype)


def _mla_attention(q, k, v, kc, vc, p1, bsz, n, nc):
    tq = _pick(n, 512)
    tk = _pick(n, 512)
    kd = 2 * LANE
    return pl.pallas_call(
        functools.partial(_mla_attn_kernel, n=n, nc=nc, tk=tk),
        grid=(bsz, HEADS, n // tq),
        in_specs=[pl.BlockSpec((1, tq, kd), lambda b, h, i: (b, i, h)),
                  pl.BlockSpec((1, n, kd), lambda b, h, i: (b, 0, h)),
                  pl.BlockSpec((1, n, kd), lambda b, h, i: (b, 0, h)),
                  pl.BlockSpec((1, nc, kd), lambda b, h, i: (b, 0, h)),
                  pl.BlockSpec((1, nc, kd), lambda b, h, i: (b, 0, h)),
                  pl.BlockSpec((1, tq, LANE), lambda b, h, i: (b, i, BLK_C_GATE + h))],
        out_specs=pl.BlockSpec((1, tq, LANE), lambda b, h, i: (b, i, h)),
        out_shape=jax.ShapeDtypeStruct((bsz, n, BR_W), CDT),
        scratch_shapes=[pltpu.VMEM((tq, n + nc), F32)],
        compiler_params=_cparams(("parallel", "parallel", "arbitrary")),
        name="mla_attention",
    )(q, k, v, kc, vc, p1)


def _conv_kernel(x_ref, w_ref, o_ref, buf, *, n, k_scale):
    cb = pl.program_id(1)
    front = CONV_K // 2
    pad = 8
    buf[pl.ds(0, pad), :] = jnp.zeros((pad, LANE), F32)
    buf[pl.ds(pad + n, pad), :] = jnp.zeros((pad, LANE), F32)
    buf[pl.ds(pad, n), :] = x_ref[0].astype(F32)
    acc = None
    for j in range(CONV_K):
        term = buf[pl.ds(pad + j - front, n), :] * w_ref[j:j + 1, :]
        acc = term if acc is None else acc + term
    y = _silu(acc) * jnp.where(cb >= HEADS, k_scale, 1.0)
    o_ref[0] = y.astype(o_ref.dtype)


def _short_conv(p1, conv_w):
    bsz, n, _ = p1.shape
    ncb = 2 * HEADS
    return pl.pallas_call(
        functools.partial(_conv_kernel, n=n, k_scale=HEAD_DIM ** -0.5),
        grid=(bsz, ncb),
        in_specs=[pl.BlockSpec((1, n, LANE), lambda b, c: (b, 0, BLK_D_Q + c)),
                  pl.BlockSpec((CONV_K, LANE), lambda b, c: (0, c))],
        out_specs=pl.BlockSpec((1, n, LANE), lambda b, c: (b, 0, c)),
        out_shape=jax.ShapeDtypeStruct((bsz, n, 2 * BR_W), CDT),
        scratch_shapes=[pltpu.VMEM((n + 16, LANE), F32)],
        compiler_params=_cparams(("parallel", "arbitrary")),
        name="mlstm_conv",
    )(p1, conv_w)


def _scan_rows(x, op, fill, reverse):
    n = x.shape[0]
    t = lax.broadcasted_iota(jnp.int32, x.shape, 0)
    k = 1
    while k < n:
        if reverse:
            sh = jnp.where(t < n - k, pltpu.roll(x, n - k, 0), fill)
        else:
            sh = jnp.where(t >= k, pltpu.roll(x, k, 0), fill)
        x = op(x, sh)
        k *= 2
    return x


def _mlstm_kernel(qkc_ref, vc_ref, gc_ref, qkf_ref, vf_ref, gf_ref, qkb_ref, vb_ref, gb_ref, bif_ref,
                  hf_ref, hb_ref, c_ref, m_ref):
    j = pl.program_id(1)
    L = ML_CHUNK
    first = j == 0

    @pl.when(j == 0)
    def _():
        c_ref[...] = jnp.zeros(c_ref.shape, F32)
        m_ref[...] = jnp.zeros(m_ref.shape, F32)

    row = lax.broadcasted_iota(jnp.int32, (L, L), 0)
    col = lax.broadcasted_iota(jnp.int32, (L, L), 1)
    ones_col = jnp.where(lax.broadcasted_iota(jnp.int32, (L, LANE), 1) == 0, 1.0, 0.0).astype(CDT)

    for d, (qk_ref, v_ref, g_ref, h_ref) in enumerate(((qkf_ref, vf_ref, gf_ref, hf_ref),
                                                         (qkb_ref, vb_ref, gb_ref, hb_ref))):
        rev = d == 1
        qk_all = jnp.where(first, qkc_ref[0], qk_ref[0])
        v_all = jnp.where(first, vc_ref[0], v_ref[0])
        g = jnp.where(first, gc_ref[0], g_ref[0]) + bif_ref[...]
        li = g[:, :LANE]
        lf = jax.nn.log_sigmoid(g[:, LANE:])
        bc = _scan_rows(lf, jnp.add, 0.0, rev)
        r = li - bc
        cm = _scan_rows(r, jnp.maximum, NEG, rev)
        m_row = m_ref[d, 0:1, :]
        big_m = jnp.maximum(m_row, cm)
        a_all = jnp.exp(m_row - big_m)
        floor_all = jnp.exp(-(bc + big_m))
        m_fin = jnp.maximum(m_row, jnp.max(r, axis=0, keepdims=True))
        wk_all = jnp.exp(r - m_fin)
        decay_all = jnp.exp(m_row - m_fin)
        b_last = bc[0:1, :] if rev else bc[L - 1:L, :]
        m_ref[d, 0:1, :] = b_last + m_fin
        r_t = r.T
        causal = (col >= row) if rev else (col <= row)

        for h in range(HEADS):
            c = d * HEADS + h
            q = qk_all[:, h * LANE:(h + 1) * LANE]
            k = qk_all[:, (HEADS + h) * LANE:(HEADS + h + 1) * LANE]
            v_ext = jnp.concatenate([v_all[:, h * LANE:(h + 1) * LANE], ones_col], axis=1)
            dm = jnp.exp(jnp.where(causal, r_t[c:c + 1, :] - big_m[:, c:c + 1], NEG))
            s = (_qk(q, k) * dm).astype(CDT)
            c_old = c_ref[c]
            nd = (a_all[:, c:c + 1] * jnp.dot(q, c_old.astype(CDT), preferred_element_type=F32)
                  + jnp.dot(s, v_ext, preferred_element_type=F32))
            den = jnp.maximum(jnp.abs(nd[:, LANE:LANE + 1]), floor_all[:, c:c + 1])
            h_ref[0, :, h * LANE:(h + 1) * LANE] = nd[:, :LANE] / den
            kw_t = (k.astype(F32) * wk_all[:, c:c + 1]).T.astype(CDT)
            upd = jnp.dot(kw_t, v_ext, preferred_element_type=F32)
            c_ref[c] = decay_all[:, c:c + 1] * c_old + upd


def _mlstm_scan(qk_l, p1, g_l, qk_c, p1c, g_c, b_if2):
    L = ML_CHUNK
    bsz, n, _ = qk_l.shape
    nc = qk_c.shape[1]
    nl = n // L
    steps = nl + 1
    vblk = BLK_D_V * LANE // BR_W
    fwd = lambda j: jnp.maximum(j - 1, 0)
    bwd = lambda j: jnp.minimum(nl - j, nl - 1)
    return pl.pallas_call(
        _mlstm_kernel,
        grid=(bsz, steps),
        in_specs=[pl.BlockSpec((1, L, 2 * BR_W), lambda b, j: (b, 0, 0)),
                  pl.BlockSpec((1, L, BR_W), lambda b, j: (b, 0, vblk)),
                  pl.BlockSpec((1, L, 2 * LANE), lambda b, j: (b, 0, 0)),
                  pl.BlockSpec((1, L, 2 * BR_W), lambda b, j: (b, fwd(j), 0)),
                  pl.BlockSpec((1, L, BR_W), lambda b, j: (b, fwd(j), vblk)),
                  pl.BlockSpec((1, L, 2 * LANE), lambda b, j: (b, fwd(j), 0)),
                  pl.BlockSpec((1, L, 2 * BR_W), lambda b, j: (b, bwd(j), 0)),
                  pl.BlockSpec((1, L, BR_W), lambda b, j: (b, bwd(j), vblk)),
                  pl.BlockSpec((1, L, 2 * LANE), lambda b, j: (b, bwd(j), 0)),
                  pl.BlockSpec((1, 2 * LANE), lambda b, j: (0, 0))],
        out_specs=[pl.BlockSpec((1, L, BR_W), lambda b, j: (b, j, 0)),
                   pl.BlockSpec((1, L, BR_W), lambda b, j: (b, steps - 1 - j, 0))],
        out_shape=[jax.ShapeDtypeStruct((bsz, n + nc, BR_W), F32),
                   jax.ShapeDtypeStruct((bsz, n + nc, BR_W), F32)],
        scratch_shapes=[pltpu.VMEM((2 * HEADS, LANE, 2 * LANE), F32),
                        pltpu.VMEM((2, 8, LANE), F32)],
        compiler_params=_cparams(("parallel", "arbitrary")),
        name="mlstm_scan",
    )(qk_c, p1c, g_c, qk_l, p1, g_l, qk_l, p1, g_l, b_if2)


def _merge_kernel(x_ref, mod_ref, ya, yb, yc, hf_ref, hb_ref, do_ref, dg_ref, gn_ref, mp_ref, wbr_ref, wout_ref,
                  gfin_ref, o_ref, *, d, final_norm):
    parts = []
    for h in range(HEADS):
        sl = slice(h * LANE, (h + 1) * LANE)
        hh = hf_ref[0, :, sl] + hb_ref[0, :, sl]
        hn = hh * lax.rsqrt(jnp.mean(hh * hh, axis=-1, keepdims=True) + EPS)
        y = hn * gn_ref[:, sl] * jax.nn.sigmoid(do_ref[0, :, sl].astype(F32))
        parts.append((y * _silu(dg_ref[0, :, sl].astype(F32))).astype(CDT))
    yd = jnp.concatenate(parts, axis=1)

    acc = None
    for i, y in enumerate((ya[...], yb[...], yc[...], yd)):
        z = jnp.dot(y, wbr_ref[i], preferred_element_type=F32)
        t = jax.nn.sigmoid(mp_ref[:, i * d:(i + 1) * d].astype(F32)) * z
        acc = t if acc is None else acc + t
    out = jnp.dot(acc.astype(CDT), wout_ref[...], preferred_element_type=F32)
    x = x_ref[...] + mod_ref[0, 2:3, :] * out
    if final_norm:
        x = x * lax.rsqrt(jnp.mean(x * x, axis=-1, keepdims=True) + EPS) * gfin_ref[...]
    o_ref[...] = x


def _merge(x2, mod, mod_row, ys, h_f, h_b, f_off, b_off, p1, gnorm, mp, w_br, w_out, g_final, final_norm):
    t, d = x2.shape
    tm = ML_CHUNK
    per = p1.shape[1] // tm
    wblk = BR_W // LANE
    y_spec = pl.BlockSpec((tm, BR_W), lambda i: (i, 0))
    return pl.pallas_call(
        functools.partial(_merge_kernel, d=d, final_norm=final_norm),
        grid=(t // tm,),
        in_specs=[pl.BlockSpec((tm, d), lambda i: (i, 0)),
                  pl.BlockSpec((1, 3, d), lambda i: (mod_row(i), 0, 0)),
                  y_spec, y_spec, y_spec,
                  pl.BlockSpec((1, tm, BR_W), lambda i: (i // per, f_off + i % per, 0)),
                  pl.BlockSpec((1, tm, BR_W), lambda i: (i // per, b_off + i % per, 0)),
                  pl.BlockSpec((1, tm, BR_W), lambda i: (i // per, i % per, BLK_D_O // wblk)),
                  pl.BlockSpec((1, tm, BR_W), lambda i: (i // per, i % per, BLK_D_GATE // wblk)),
                  pl.BlockSpec((1, BR_W), lambda i: (0, 0)),
                  pl.BlockSpec((tm, N_BRANCH * d), lambda i: (i, 0)),
                  pl.BlockSpec((N_BRANCH, BR_W, d), lambda i: (0, 0, 0), pipeline_mode=pl.Buffered(1)),
                  pl.BlockSpec((d, d), lambda i: (0, 0), pipeline_mode=pl.Buffered(1)),
                  pl.BlockSpec((1, d), lambda i: (0, 0))],
        out_specs=pl.BlockSpec((tm, d), lambda i: (i, 0)),
        out_shape=jax.ShapeDtypeStruct((t, d), F32),
        compiler_params=_cparams(("parallel",)),
        name="merge",
    )(x2, mod, *ys, h_f, h_b, p1, p1, gnorm.reshape(1, BR_W), mp, w_br, w_out, g_final.reshape(1, d))


def _split_offsets(d):
    widths = (3 * BR_W, BR_W, BR_W, BR_W, MLA_Q_LORA, MLA_KV_LORA, MLA_ROPE, BR_W,
              3 * BR_W, BR_W, 4 * HEADS, BR_W, N_BRANCH * d)
    names = ('a_qkv', 'a_gate', 'b_in', 'b_gate', 'c_q', 'c_kv', 'c_kr', 'c_gate',
             'd_qkv', 'd_o', 'd_if', 'd_gate', 'merge')
    offs = np.concatenate([[0], np.cumsum(widths)])
    return {nm: (int(offs[i]), int(offs[i + 1])) for i, nm in enumerate(names)}


def _rope_swap_perm():
    half = MLA_ROPE // 2
    q = half // 2
    return np.concatenate([np.arange(q, half), np.arange(0, q), np.arange(half + q, MLA_ROPE), np.arange(half, half + q)])


def _prep_w_in(w_in):
    nl, d, _ = w_in.shape
    o = _split_offsets(d)
    w_in = w_in.astype(CDT)
    cut = lambda nm: w_in[:, :, o[nm][0]:o[nm][1]]
    kr = cut('c_kr')
    d_if = cut('d_if')
    d_if = d_if.reshape(nl, d, 2, 2, HEADS)
    zpad = jnp.zeros((nl, d, LANE - 2 * HEADS), w_in.dtype)
    li = jnp.concatenate([d_if[:, :, :, 0, :].reshape(nl, d, 2 * HEADS), zpad], axis=-1)
    lf = jnp.concatenate([d_if[:, :, :, 1, :].reshape(nl, d, 2 * HEADS), zpad], axis=-1)
    w1 = jnp.concatenate([
        w_in[:, :, :o['c_kv'][1]], cut('c_gate'), cut('d_qkv'), cut('d_o'), cut('d_gate'),
        kr, kr[:, :, _rope_swap_perm()], jnp.zeros((nl, d, (BLK_D_LI - BLK_C_KR - 1) * LANE), w_in.dtype),
        li, lf], axis=-1)
    assert w1.shape[-1] == N1, w1.shape
    return w1.astype(CDT), cut('merge').astype(CDT)


def _prep_b_if(b_if):
    nl = b_if.shape[0]
    b = b_if.reshape(nl, 2, 2, HEADS)
    zpad = jnp.zeros((nl, LANE - 2 * HEADS), b_if.dtype)
    li = jnp.concatenate([b[:, :, 0, :].reshape(nl, 2 * HEADS), zpad], axis=-1)
    lf = jnp.concatenate([b[:, :, 1, :].reshape(nl, 2 * HEADS), zpad], axis=-1)
    return jnp.concatenate([li, lf], axis=-1).reshape(nl, 1, 2 * LANE)


def _prep_w_uq(w_uq):
    nl, r, _ = w_uq.shape
    w = w_uq.reshape(nl, r, HEADS, MLA_NOPE + MLA_ROPE)
    rope = w[..., MLA_NOPE:]
    w = jnp.concatenate([w[..., :MLA_NOPE], rope, rope[..., _rope_swap_perm()]], axis=-1)
    return w.reshape(nl, r, HEADS * 2 * LANE).astype(CDT)


def kernel(x, c, ctx, c_ctx, norm_g, w_mod, b_mod, w_in, na_rpb, pool_w, pool_scale, mla_gq, mla_gkv, w_uq, w_ukv,
           conv_w, b_if, ml_gnorm, w_br, w_out, g_final):
    bsz, n, d = x.shape
    nc = ctx.shape[1]
    depth = w_in.shape[0]
    rows = n // GRID_W
    assert n % (NA_QROWS * GRID_W) == 0 and rows >= NA_KROWS
    assert nc == ML_CHUNK and n % ML_CHUNK == 0 and nc % 8 == 0

    w1_all, w2_all = _prep_w_in(w_in)
    wq_all = _prep_w_uq(w_uq)
    wkv_all = w_ukv.astype(CDT)
    wbr_all = w_br.astype(CDT)
    wout_all = w_out.astype(CDT)
    poolw_all = pool_w.astype(CDT)
    bif_all = _prep_b_if(b_if)
    bias_all = _na_column_tables(na_rpb)
    tab_lat = _rope_tables(n, False)
    tab_ctx = _rope_tables(nc, True)

    n_cond = -(-(bsz + 1) // 8) * 8
    cond = jnp.concatenate([c, c_ctx[None, :], jnp.zeros((n_cond - bsz - 1, d), F32)], axis=0)
    mods = _modulation(cond, w_mod, b_mod)
    mods = mods.reshape(depth, n_cond, 3, d)

    tm_l = _pick(n, 1024)
    tm_c = _pick(bsz * nc, 1024)
    tn1 = _pick(N1, 1536) if N1 % 1536 == 0 else _pick(N1, 512)
    tn2 = _pick(N_BRANCH * d, 2048)
    lat_row = lambda i: i // (n // tm_l)
    ctx_row = lambda i: bsz
    lat_row_m = lambda i: i // (n // ML_CHUNK)

    xl = x.reshape(bsz * n, d)
    xc = ctx.reshape(bsz * nc, d)
    for l in range(depth):
        need_ctx = l < depth - 1
        last = l == depth - 1
        mod = mods[l]
        p1, gates = _inproj(xl, mod, lat_row, norm_g[l], w1_all[l], tm_l, tn1, 2 * LANE)
        p2 = _inproj(xl, mod, lat_row, norm_g[l], w2_all[l], tm_l, tn2, 0)[0]
        p1c, gates_c = _inproj(xc, mod, ctx_row, norm_g[l], w1_all[l], tm_c, tn1, 2 * LANE)
        p1_3 = p1.reshape(bsz, n, N1)
        p1c_3 = p1c.reshape(bsz, nc, N1)

        ya = _neighbourhood(p1_3, p1c_3, bias_all[l], bsz, n, nc)
        yb = _pool(p1_3, poolw_all[l], pool_scale[l])
        q_l, k_l, v_l = _mla_proj(p1, n, mla_gq[l], mla_gkv[l], wq_all[l], wkv_all[l], tab_lat)
        q_c, k_c, v_c = _mla_proj(p1c, nc, mla_gq[l], mla_gkv[l], wq_all[l], wkv_all[l], tab_ctx)
        r3 = lambda a, m: a.reshape(bsz, m, a.shape[-1])
        yc = _mla_attention(r3(q_l, n), r3(k_l, n), r3(v_l, n), r3(k_c, nc), r3(v_c, nc), p1_3, bsz, n, nc)
        qk_l = _short_conv(p1_3, conv_w[l])
        qk_c = _short_conv(p1c_3, conv_w[l])
        g_l = gates.reshape(bsz, n, 2 * LANE)
        g_c = gates_c.reshape(bsz, nc, 2 * LANE)
        h_f, h_b = _mlstm_scan(qk_l, p1_3, g_l, qk_c, p1c_3, g_c, bif_all[l])

        f2 = lambda a: a.reshape(-1, a.shape[-1])
        xl_new = _merge(xl, mod, lat_row_m, (f2(ya), f2(yb), f2(yc)), h_f, h_b, nc // ML_CHUNK, 0, p1_3, ml_gnorm[l],
                        p2, wbr_all[l], wout_all[l], g_final, last)
        if need_ctx:
            p2c = _inproj(xc, mod, ctx_row, norm_g[l], w2_all[l], tm_c, tn2, 0)[0]
            ya_c = _ctx_attention(p1c_3, BLK_A_Q, LANE, p1c_3, BLK_A_K, LANE, p1c_3, BLK_A_V, 1, p1c_3, BLK_A_GATE,
                                  HEAD_DIM ** -0.5, False)
            yb_c = _pool(p1c_3, poolw_all[l], pool_scale[l])
            yc_c = _ctx_attention(r3(q_c, nc), 0, 2 * LANE, r3(k_c, nc), 0, 2 * LANE, r3(v_c, nc), 0, 2,
                                  p1c_3, BLK_C_GATE, 1.0, True)
            xc = _merge(xc, mod, ctx_row, (f2(ya_c), f2(yb_c), f2(yc_c)), h_f, h_b, 0, n // ML_CHUNK, p1c_3, ml_gnorm[l],
                        p2c, wbr_all[l], wout_all[l], g_final, False)
        xl = xl_new
    return xl.reshape(bsz, n, d)
```

```python
import functools

import numpy as np
import jax
import jax.numpy as jnp
from jax import lax
from jax.experimental import pallas as pl
from jax.experimental.pallas import tpu as pltpu

F32 = jnp.float32
CDT = jnp.bfloat16

GRID_W = 64
EPS = 1e-6
N_BRANCH = 4
BR_W = 512
HEADS = 4
HEAD_DIM = BR_W // HEADS
NA_ROWS = 8
NA_COLS = 16
POOL_WINDOWS = (2, 4, 8, 16)
MLA_NOPE = 128
MLA_ROPE = 64
MLA_Q_LORA = 512
MLA_KV_LORA = 512
ROPE_THETA = 10000.0
CONV_K = 4

LANE = 128
VMEM_LIMIT = 56 * 1024 * 1024
NEG = -1e30

BLK_A_Q, BLK_A_K, BLK_A_V, BLK_A_GATE = 0, 4, 8, 12
BLK_B_IN, BLK_B_GATE = 16, 20
BLK_C_Q, BLK_C_KV, BLK_C_GATE = 24, 28, 32
BLK_D_Q, BLK_D_K, BLK_D_V, BLK_D_O, BLK_D_GATE = 36, 40, 44, 48, 52
BLK_C_KR = 56
BLK_D_LI, BLK_D_LF = 58, 59
N1_BLOCKS = 60
N1 = N1_BLOCKS * LANE

NA_QROWS = 8
NA_KROWS = 16
ML_CHUNK = 256
MLA_ROW_SPLIT = 2


def _cparams(sem):
    return pltpu.CompilerParams(dimension_semantics=sem, vmem_limit_bytes=VMEM_LIMIT)


def _silu(x):
    return x * jax.nn.sigmoid(x)


def _pick(n, pref):
    t = min(n, pref)
    while n % t:
        t //= 2
    return t


def _mod_kernel(c_ref, w_ref, b_ref, o_ref):
    c = _silu(c_ref[...]).astype(CDT)
    o_ref[0] = jnp.dot(c, w_ref[0].astype(CDT), preferred_element_type=F32) + b_ref[0]


def _modulation(cond, w_mod, b_mod):
    nl, d, d3 = w_mod.shape
    r = cond.shape[0]
    tn = _pick(d3, 1024)
    return pl.pallas_call(
        _mod_kernel,
        grid=(nl, d3 // tn),
        in_specs=[pl.BlockSpec((r, d), lambda l, j: (0, 0)),
                  pl.BlockSpec((1, d, tn), lambda l, j: (l, 0, j)),
                  pl.BlockSpec((1, 1, tn), lambda l, j: (l, 0, j))],
        out_specs=pl.BlockSpec((1, r, tn), lambda l, j: (l, 0, j)),
        out_shape=jax.ShapeDtypeStruct((nl, r, d3), F32),
        compiler_params=_cparams(("parallel", "parallel")),
        name="modulation",
    )(cond, w_mod, b_mod.reshape(nl, 1, d3))


def _norm_kernel(x_ref, mod_ref, g_ref, o_ref):
    x = x_ref[...]
    y = x * lax.rsqrt(jnp.mean(x * x, axis=-1, keepdims=True) + EPS) * g_ref[...]
    o_ref[...] = (y * (1.0 + mod_ref[0, 1:2, :]) + mod_ref[0, 0:1, :]).astype(o_ref.dtype)


def _mod_norm(x2, mod, mod_row, norm_g, tm):
    t, d = x2.shape
    return pl.pallas_call(
        _norm_kernel,
        grid=(t // tm,),
        in_specs=[pl.BlockSpec((tm, d), lambda i: (i, 0)),
                  pl.BlockSpec((1, 3, d), lambda i: (mod_row(i), 0, 0)),
                  pl.BlockSpec((1, d), lambda i: (0, 0))],
        out_specs=pl.BlockSpec((tm, d), lambda i: (i, 0)),
        out_shape=jax.ShapeDtypeStruct((t, d), CDT),
        compiler_params=_cparams(("parallel",)),
        name="mod_norm",
    )(x2, mod, norm_g.reshape(1, d))


def _inproj_kernel(a_ref, w_ref, o_ref, *rest, n_f32_cols):
    acc = jnp.dot(a_ref[...], w_ref[...], preferred_element_type=F32)
    o_ref[...] = acc.astype(o_ref.dtype)
    if n_f32_cols:
        f_ref = rest[0]

        @pl.when(pl.program_id(1) == pl.num_programs(1) - 1)
        def _():
            f_ref[...] = acc[:, acc.shape[1] - n_f32_cols:]


def _inproj(hn, w, tm, tn, n_f32_cols):
    t, d = hn.shape
    n = w.shape[1]
    out_shape = [jax.ShapeDtypeStruct((t, n), CDT)]
    out_specs = [pl.BlockSpec((tm, tn), lambda i, j: (i, j))]
    if n_f32_cols:
        out_shape.append(jax.ShapeDtypeStruct((t, n_f32_cols), F32))
        out_specs.append(pl.BlockSpec((tm, n_f32_cols), lambda i, j: (i, 0)))
    return pl.pallas_call(
        functools.partial(_inproj_kernel, n_f32_cols=n_f32_cols),
        grid=(t // tm, n // tn),
        in_specs=[pl.BlockSpec((tm, d), lambda i, j: (i, 0)),
                  pl.BlockSpec((d, tn), lambda i, j: (0, j))],
        out_specs=out_specs,
        out_shape=out_shape,
        compiler_params=_cparams(("parallel", "arbitrary")),
        name="inproj",
    )(hn, w)


def _softmax_pv(parts, exp=jnp.exp):
    m = None
    for s, _ in parts:
        mi = jnp.max(s, axis=-1, keepdims=True)
        m = mi if m is None else jnp.maximum(m, mi)
    l = None
    o = None
    for s, v in parts:
        p = exp(s - m)
        li = jnp.sum(p, axis=-1, keepdims=True)
        oi = jnp.dot(p.astype(CDT), v, preferred_element_type=F32)
        l = li if l is None else l + li
        o = oi if o is None else o + oi
    return o / l


def _qk(q, k):
    return lax.dot_general(q, k, (((1,), (1,)), ((), ())), preferred_element_type=F32)


def _na_row_offsets(rows, jt):
    start = int(np.clip(jt * NA_QROWS - NA_ROWS // 2, 0, rows - NA_KROWS))
    r = jt * NA_QROWS + np.arange(NA_QROWS)
    keyrow = start + np.arange(NA_KROWS)
    rs = np.clip(r - NA_ROWS // 2, 0, rows - NA_ROWS)
    row_ok = (keyrow[None, :] >= rs[:, None]) & (keyrow[None, :] < rs[:, None] + NA_ROWS)
    return np.where(row_ok, keyrow[None, :] - r[:, None] + NA_ROWS - 1, -1)


def _na_kernel(q_ref, k_ref, v_ref, kc_ref, vc_ref, g_ref, bc_ref, o_ref, bias_ref, *, rows, scale):
    j = pl.program_id(2)
    nj = rows // NA_QROWS
    w = GRID_W
    left = lax.broadcasted_iota(jnp.int32, (w, LANE), 1) < w

    def build(jt):
        roff = _na_row_offsets(rows, jt)
        neg = jnp.full((w, LANE), NEG, F32)
        for g in range(NA_QROWS):
            for kp in range(NA_KROWS // 2):
                ra, rb = int(roff[g, 2 * kp]), int(roff[g, 2 * kp + 1])
                ta = bc_ref[0, ra] if ra >= 0 else neg
                tb = bc_ref[0, rb] if rb >= 0 else neg
                tile = neg if (ra < 0 and rb < 0) else jnp.where(left, ta, tb)
                bias_ref[g * w:(g + 1) * w, kp * LANE:(kp + 1) * LANE] = tile

    pl.when(j == 0)(lambda: build(0))
    if nj > 2:
        pl.when(j == 1)(lambda: build(1))
    pl.when(j == nj - 1)(lambda: build(nj - 1))

    start = jnp.clip(j * NA_QROWS - NA_ROWS // 2, 0, rows - NA_KROWS) * GRID_W
    start = pl.multiple_of(start, GRID_W)
    kw = k_ref[0, pl.ds(start, NA_KROWS * GRID_W), :]
    vw = v_ref[0, pl.ds(start, NA_KROWS * GRID_W), :]
    half = q_ref.shape[1] // 2
    for rs in (slice(0, half), slice(half, 2 * half)):
        q = q_ref[0, rs, :]
        s_loc = _qk(q, kw) * scale + bias_ref[rs, :]
        s_ctx = _qk(q, kc_ref[0]) * scale
        o = _softmax_pv([(s_loc, vw), (s_ctx, vc_ref[0])])
        o_ref[0, rs, :] = (o * _silu(g_ref[0, rs, :].astype(F32))).astype(o_ref.dtype)


def _na_column_tables(rpb):
    w = GRID_W
    nl, nh, n_ro, n_co = rpb.shape
    cq = np.arange(w)
    cstart = np.clip(cq - NA_COLS // 2, 0, w - NA_COLS)
    col_ok = (cq[None, :] >= cstart[:, None]) & (cq[None, :] < cstart[:, None] + NA_COLS)
    period = 2 * w + 1
    v = jnp.concatenate([rpb[..., NA_COLS - 1:], jnp.zeros((nl, nh, n_ro, period - n_co), rpb.dtype),
                         rpb[..., :NA_COLS - 1]], axis=-1).astype(F32)
    t = jnp.tile(v, (1, 1, 1, w))[..., :w * 2 * w].reshape(nl, nh, n_ro, w, 2 * w)[..., :w]
    t = jnp.where(jnp.asarray(col_ok), t, NEG)
    return jnp.concatenate([t, t], axis=-1)


def _neighbourhood(p1, p1c, bc, bsz, n, nc):
    rows = n // GRID_W
    nj = rows // NA_QROWS
    tq = NA_QROWS * GRID_W
    n_ro = bc.shape[1]
    return pl.pallas_call(
        functools.partial(_na_kernel, rows=rows, scale=HEAD_DIM ** -0.5),
        grid=(bsz, HEADS, nj),
        in_specs=[pl.BlockSpec((1, tq, LANE), lambda b, h, j: (b, j, BLK_A_Q + h)),
                  pl.BlockSpec((1, n, LANE), lambda b, h, j: (b, 0, BLK_A_K + h)),
                  pl.BlockSpec((1, n, LANE), lambda b, h, j: (b, 0, BLK_A_V + h)),
                  pl.BlockSpec((1, nc, LANE), lambda b, h, j: (b, 0, BLK_A_K + h)),
                  pl.BlockSpec((1, nc, LANE), lambda b, h, j: (b, 0, BLK_A_V + h)),
                  pl.BlockSpec((1, tq, LANE), lambda b, h, j: (b, j, BLK_A_GATE + h)),
                  pl.BlockSpec((1, n_ro, GRID_W, 2 * GRID_W), lambda b, h, j: (h, 0, 0, 0))],
        out_specs=pl.BlockSpec((1, tq, LANE), lambda b, h, j: (b, j, h)),
        out_shape=jax.ShapeDtypeStruct((bsz, n, BR_W), CDT),
        scratch_shapes=[pltpu.VMEM((tq, NA_KROWS * GRID_W), F32)],
        compiler_params=_cparams(("parallel", "parallel", "arbitrary")),
        name="neighbourhood_attention",
    )(p1, p1, p1, p1c, p1c, p1, bc)


def _ctx_attn_kernel(q_ref, k_ref, v_ref, g_ref, o_ref, *, scale, base2):
    s = _qk(q_ref[0], k_ref[0]) * scale
    o = _softmax_pv([(s, v_ref[0])], jnp.exp2 if base2 else jnp.exp)
    o_ref[0] = (o * _silu(g_ref[0].astype(F32))).astype(o_ref.dtype)


def _ctx_attention(q, qblk, qw, k, kblk, kw, v, vblk, vstep, gate, gblk, scale, base2):
    bsz, nc, _ = q.shape
    return pl.pallas_call(
        functools.partial(_ctx_attn_kernel, scale=scale, base2=base2),
        grid=(bsz, HEADS),
        in_specs=[pl.BlockSpec((1, nc, qw), lambda b, h: (b, 0, qblk + h)),
                  pl.BlockSpec((1, nc, kw), lambda b, h: (b, 0, kblk + h)),
                  pl.BlockSpec((1, nc, LANE), lambda b, h: (b, 0, vblk + vstep * h)),
                  pl.BlockSpec((1, nc, LANE), lambda b, h: (b, 0, gblk + h))],
        out_specs=pl.BlockSpec((1, nc, LANE), lambda b, h: (b, 0, h)),
        out_shape=jax.ShapeDtypeStruct((bsz, nc, BR_W), CDT),
        compiler_params=_cparams(("parallel", "parallel")),
        name="context_attention",
    )(q, k, v, gate)


POOL_PAD = 16
POOL_HALO = 8


def _pool_kernel(u_ref, g_ref, w_ref, sc_ref, o_ref, buf_a, buf_b, *, n):
    gi = pl.program_id(1)
    x = u_ref[0].astype(F32)
    t = lax.broadcasted_iota(jnp.int32, (n, LANE), 0)
    ext = n + 2 * POOL_HALO

    def finish(wsum, w):
        lo = jnp.maximum(t - w // 2, 0)
        hi = jnp.minimum(t + (w - 1 - w // 2), n - 1)
        cnt = (hi - lo + 1).astype(F32)
        d = wsum / cnt - x
        y = jnp.dot(d.astype(CDT), w_ref[0], preferred_element_type=F32) * sc_ref[...]
        o_ref[0] = (y * _silu(g_ref[0].astype(F32))).astype(o_ref.dtype)

    def window(ref, off):
        return ref[pl.ds(POOL_PAD + off, n), :]

    def wide(ref, off):
        return ref[pl.ds(POOL_PAD - POOL_HALO + off, ext), :]

    for buf in (buf_a, buf_b):
        buf[pl.ds(0, POOL_PAD), :] = jnp.zeros((POOL_PAD, LANE), F32)
        buf[pl.ds(POOL_PAD + n, POOL_PAD), :] = jnp.zeros((POOL_PAD, LANE), F32)
    buf_a[pl.ds(POOL_PAD, n), :] = x

    for level, w in enumerate(POOL_WINDOWS):
        @pl.when(gi == level)
        def _(level=level, w=w):
            if level == 0:
                finish(window(buf_a, -1) + window(buf_a, 0), w)
                return
            buf_b[pl.ds(POOL_PAD - POOL_HALO, ext), :] = wide(buf_a, -1) + wide(buf_a, 0)
            src, dst = buf_b, buf_a
            for lv in range(1, level):
                h = 1 << (lv - 1)
                dst[pl.ds(POOL_PAD - POOL_HALO, ext), :] = wide(src, -h) + wide(src, h)
                src, dst = dst, src
            h = 1 << (level - 1)
            finish(window(src, -h) + window(src, h), w)


def _pool(p1, pool_w, pool_scale):
    bsz, n, _ = p1.shape
    ng = len(POOL_WINDOWS)
    return pl.pallas_call(
        functools.partial(_pool_kernel, n=n),
        grid=(bsz, ng),
        in_specs=[pl.BlockSpec((1, n, LANE), lambda b, g: (b, 0, BLK_B_IN + g)),
                  pl.BlockSpec((1, n, LANE), lambda b, g: (b, 0, BLK_B_GATE + g)),
                  pl.BlockSpec((1, LANE, LANE), lambda b, g: (g, 0, 0)),
                  pl.BlockSpec((1, LANE), lambda b, g: (0, g))],
        out_specs=pl.BlockSpec((1, n, LANE), lambda b, g: (b, 0, g)),
        out_shape=jax.ShapeDtypeStruct((bsz, n, BR_W), CDT),
        scratch_shapes=[pltpu.VMEM((n + 2 * POOL_PAD, LANE), F32),
                        pltpu.VMEM((n + 2 * POOL_PAD, LANE), F32)],
        compiler_params=_cparams(("parallel", "arbitrary")),
        name="pool_branch",
    )(p1, p1, pool_w, pool_scale.reshape(1, BR_W))


def _rms(x, g):
    return x * lax.rsqrt(jnp.mean(x * x, axis=-1, keepdims=True) + EPS) * g


def _mla_proj_kernel(cq_ref, ckv_ref, kr_ref, gq_ref, gkv_ref, wq_ref, wkv_ref,
                     tq_c, tq_s, tk_c, tk_s, q_ref, k_ref, v_ref, *, scale):
    hq = _rms(cq_ref[...].astype(F32), gq_ref[...]).astype(CDT)
    hkv = _rms(ckv_ref[...].astype(F32), gkv_ref[...]).astype(CDT)
    q = jnp.dot(hq, wq_ref[...], preferred_element_type=F32) * scale
    kv = jnp.dot(hkv, wkv_ref[...], preferred_element_type=F32)
    kr = kr_ref[...].astype(F32)
    k_rot = (kr * tk_c[...] + pltpu.roll(kr, LANE // 2, 1) * tk_s[...]).astype(CDT)
    ones_col = jnp.where(lax.broadcasted_iota(jnp.int32, kr.shape, 1) == 0, 1.0, 0.0).astype(CDT)
    for h in range(HEADS):
        lo = 2 * h * LANE
        q_rope = q[:, lo + LANE:lo + 2 * LANE]
        q_rot = q_rope * tq_c[...] + pltpu.roll(q_rope, LANE // 2, 1) * tq_s[...]
        q_ref[:, lo:lo + LANE] = q[:, lo:lo + LANE].astype(CDT)
        q_ref[:, lo + LANE:lo + 2 * LANE] = q_rot.astype(CDT)
        k_ref[:, lo:lo + LANE] = kv[:, lo:lo + LANE].astype(CDT)
        k_ref[:, lo + LANE:lo + 2 * LANE] = k_rot
        v_ref[:, lo:lo + LANE] = kv[:, lo + LANE:lo + 2 * LANE].astype(CDT)
        v_ref[:, lo + LANE:lo + 2 * LANE] = ones_col


def _rope_tables(n, is_ctx):
    half = MLA_ROPE // 2
    nfreq = half // 2
    inv = ROPE_THETA ** (-np.arange(nfreq, dtype=np.float64) / nfreq)
    pos = np.arange(n)
    ang_r = (pos // GRID_W)[:, None] * inv
    ang_c = (pos % GRID_W)[:, None] * inv
    cos = np.concatenate([np.cos(ang_r), np.cos(ang_r), np.cos(ang_c), np.cos(ang_c)], axis=1)
    sin = np.concatenate([-np.sin(ang_r), np.sin(ang_r), -np.sin(ang_c), np.sin(ang_c)], axis=1)
    zero = np.zeros_like(cos)
    one = np.ones_like(cos)
    tq_c = np.concatenate([cos, zero], axis=1)
    tq_s = np.concatenate([sin, one], axis=1)
    if is_ctx:
        tk_c = np.concatenate([zero, zero], axis=1)
        tk_s = np.concatenate([zero, one], axis=1)
    else:
        tk_c = np.concatenate([cos, zero], axis=1)
        tk_s = np.concatenate([sin, zero], axis=1)
    return tuple(jnp.asarray(a, F32) for a in (tq_c, tq_s, tk_c, tk_s))


def _mla_proj(p1f, n, gq, gkv, wq, wkv, tables):
    t = p1f.shape[0]
    tm = _pick(n, 512)
    per_seq = n // tm
    wide = HEADS * 2 * LANE
    tab_spec = pl.BlockSpec((tm, LANE), lambda i: (i % per_seq, 0))
    return pl.pallas_call(
        functools.partial(_mla_proj_kernel, scale=(MLA_NOPE + MLA_ROPE) ** -0.5 * np.log2(np.e)),
        grid=(t // tm,),
        in_specs=[pl.BlockSpec((tm, MLA_Q_LORA), lambda i: (i, BLK_C_Q * LANE // MLA_Q_LORA)),
                  pl.BlockSpec((tm, MLA_KV_LORA), lambda i: (i, BLK_C_KV * LANE // MLA_KV_LORA)),
                  pl.BlockSpec((tm, LANE), lambda i: (i, BLK_C_KR)),
                  pl.BlockSpec((1, MLA_Q_LORA), lambda i: (0, 0)),
                  pl.BlockSpec((1, MLA_KV_LORA), lambda i: (0, 0)),
                  pl.BlockSpec((MLA_Q_LORA, wide), lambda i: (0, 0)),
                  pl.BlockSpec((MLA_KV_LORA, wide), lambda i: (0, 0)),
                  tab_spec, tab_spec, tab_spec, tab_spec],
        out_specs=[pl.BlockSpec((tm, wide), lambda i: (i, 0)),
                   pl.BlockSpec((tm, wide), lambda i: (i, 0)),
                   pl.BlockSpec((tm, wide), lambda i: (i, 0))],
        out_shape=[jax.ShapeDtypeStruct((t, wide), CDT),
                   jax.ShapeDtypeStruct((t, wide), CDT),
                   jax.ShapeDtypeStruct((t, wide), CDT)],
        compiler_params=_cparams(("parallel",)),
        name="mla_projection",
    )(p1f, p1f, p1f, gq.reshape(1, -1), gkv.reshape(1, -1), wq, wkv, *tables)


def _mla_attn_kernel(q_ref, k_ref, v_ref, kc_ref, vc_ref, g_ref, o_ref, s_ref, *, n, nc, tk):
    tq = q_ref.shape[1]
    chunks = [(kc_ref, vc_ref, 0, nc, 0)] + [(k_ref, v_ref, c * tk, tk, nc + c * tk) for c in range(n // tk)]

    for rows in (slice(r, r + tq // MLA_ROW_SPLIT) for r in range(0, tq, tq // MLA_ROW_SPLIT)):
        q = q_ref[0, rows, :]
        mv = jnp.full((q.shape[0], LANE), NEG, F32)
        for kr, _, start, size, col in chunks:
            s = _qk(q, kr[0, pl.ds(start, size), :])
            s_ref[rows, col:col + size] = s
            for i in range(size // LANE):
                mv = jnp.maximum(mv, s[:, i * LANE:(i + 1) * LANE])
        m = jnp.max(mv, axis=-1, keepdims=True)

        acc = jnp.zeros((q.shape[0], 2 * LANE), F32)
        for _, vr, start, size, col in chunks:
            p = jnp.exp2(s_ref[rows, col:col + size] - m).astype(CDT)
            acc = acc + jnp.dot(p, vr[0, pl.ds(start, size), :], preferred_element_type=F32)
        o = acc[:, :LANE] / acc[:, LANE:LANE + 1]
        o_ref[0, rows, :] = (o * _silu(g_ref[0, rows, :].astype(F32))).astype(o_ref.dtype)


def _mla_attention(q, k, v, kc, vc, p1, bsz, n, nc):
    tq = _pick(n, 512)
    tk = _pick(n, 512)
    kd = 2 * LANE
    return pl.pallas_call(
        functools.partial(_mla_attn_kernel, n=n, nc=nc, tk=tk),
        grid=(bsz, HEADS, n // tq),
        in_specs=[pl.BlockSpec((1, tq, kd), lambda b, h, i: (b, i, h)),
                  pl.BlockSpec((1, n, kd), lambda b, h, i: (b, 0, h)),
                  pl.BlockSpec((1, n, kd), lambda b, h, i: (b, 0, h)),
                  pl.BlockSpec((1, nc, kd), lambda b, h, i: (b, 0, h)),
                  pl.BlockSpec((1, nc, kd), lambda b, h, i: (b, 0, h)),
                  pl.BlockSpec((1, tq, LANE), lambda b, h, i: (b, i, BLK_C_GATE + h))],
        out_specs=pl.BlockSpec((1, tq, LANE), lambda b, h, i: (b, i, h)),
        out_shape=jax.ShapeDtypeStruct((bsz, n, BR_W), CDT),
        scratch_shapes=[pltpu.VMEM((tq, n + nc), F32)],
        compiler_params=_cparams(("parallel", "parallel", "arbitrary")),
        name="mla_attention",
    )(q, k, v, kc, vc, p1)


def _conv_kernel(x_ref, w_ref, o_ref, buf, *, n, k_scale):
    cb = pl.program_id(1)
    front = CONV_K // 2
    pad = 8
    buf[pl.ds(0, pad), :] = jnp.zeros((pad, LANE), F32)
    buf[pl.ds(pad + n, pad), :] = jnp.zeros((pad, LANE), F32)
    buf[pl.ds(pad, n), :] = x_ref[0].astype(F32)
    acc = None
    for j in range(CONV_K):
        term = buf[pl.ds(pad + j - front, n), :] * w_ref[j:j + 1, :]
        acc = term if acc is None else acc + term
    y = _silu(acc) * jnp.where(cb >= HEADS, k_scale, 1.0)
    o_ref[0] = y.astype(o_ref.dtype)


def _short_conv(p1, conv_w):
    bsz, n, _ = p1.shape
    ncb = 2 * HEADS
    return pl.pallas_call(
        functools.partial(_conv_kernel, n=n, k_scale=HEAD_DIM ** -0.5),
        grid=(bsz, ncb),
        in_specs=[pl.BlockSpec((1, n, LANE), lambda b, c: (b, 0, BLK_D_Q + c)),
                  pl.BlockSpec((CONV_K, LANE), lambda b, c: (0, c))],
        out_specs=pl.BlockSpec((1, n, LANE), lambda b, c: (b, 0, c)),
        out_shape=jax.ShapeDtypeStruct((bsz, n, 2 * BR_W), CDT),
        scratch_shapes=[pltpu.VMEM((n + 16, LANE), F32)],
        compiler_params=_cparams(("parallel", "arbitrary")),
        name="mlstm_conv",
    )(p1, conv_w)


def _scan_rows(x, op, fill, reverse):
    n = x.shape[0]
    t = lax.broadcasted_iota(jnp.int32, x.shape, 0)
    k = 1
    while k < n:
        if reverse:
            sh = jnp.where(t < n - k, pltpu.roll(x, n - k, 0), fill)
        else:
            sh = jnp.where(t >= k, pltpu.roll(x, k, 0), fill)
        x = op(x, sh)
        k *= 2
    return x


def _mlstm_kernel(qkc_ref, vc_ref, gc_ref, qkf_ref, vf_ref, gf_ref, qkb_ref, vb_ref, gb_ref, bif_ref,
                  hf_ref, hb_ref, c_ref, m_ref):
    j = pl.program_id(1)
    L = ML_CHUNK
    first = j == 0

    @pl.when(j == 0)
    def _():
        c_ref[...] = jnp.zeros(c_ref.shape, F32)
        m_ref[...] = jnp.zeros(m_ref.shape, F32)

    row = lax.broadcasted_iota(jnp.int32, (L, L), 0)
    col = lax.broadcasted_iota(jnp.int32, (L, L), 1)
    ones_col = jnp.where(lax.broadcasted_iota(jnp.int32, (L, LANE), 1) == 0, 1.0, 0.0).astype(CDT)

    for d, (qk_ref, v_ref, g_ref, h_ref) in enumerate(((qkf_ref, vf_ref, gf_ref, hf_ref),
                                                         (qkb_ref, vb_ref, gb_ref, hb_ref))):
        rev = d == 1
        qk_all = jnp.where(first, qkc_ref[0], qk_ref[0])
        v_all = jnp.where(first, vc_ref[0], v_ref[0])
        g = jnp.where(first, gc_ref[0], g_ref[0]) + bif_ref[...]
        li = g[:, :LANE]
        lf = jax.nn.log_sigmoid(g[:, LANE:])
        bc = _scan_rows(lf, jnp.add, 0.0, rev)
        r = li - bc
        cm = _scan_rows(r, jnp.maximum, NEG, rev)
        m_row = m_ref[d, 0:1, :]
        big_m = jnp.maximum(m_row, cm)
        a_all = jnp.exp(m_row - big_m)
        floor_all = jnp.exp(-(bc + big_m))
        m_fin = jnp.maximum(m_row, jnp.max(r, axis=0, keepdims=True))
        wk_all = jnp.exp(r - m_fin)
        decay_all = jnp.exp(m_row - m_fin)
        b_last = bc[0:1, :] if rev else bc[L - 1:L, :]
        m_ref[d, 0:1, :] = b_last + m_fin
        r_t = r.T
        causal = (col >= row) if rev else (col <= row)

        for h in range(HEADS):
            c = d * HEADS + h
            q = qk_all[:, h * LANE:(h + 1) * LANE]
            k = qk_all[:, (HEADS + h) * LANE:(HEADS + h + 1) * LANE]
            v_ext = jnp.concatenate([v_all[:, h * LANE:(h + 1) * LANE], ones_col], axis=1)
            dm = jnp.exp(jnp.where(causal, r_t[c:c + 1, :] - big_m[:, c:c + 1], NEG))
            s = (_qk(q, k) * dm).astype(CDT)
            c_old = c_ref[c]
            nd = (a_all[:, c:c + 1] * jnp.dot(q, c_old.astype(CDT), preferred_element_type=F32)
                  + jnp.dot(s, v_ext, preferred_element_type=F32))
            den = jnp.maximum(jnp.abs(nd[:, LANE:LANE + 1]), floor_all[:, c:c + 1])
            h_ref[0, :, h * LANE:(h + 1) * LANE] = nd[:, :LANE] / den
            kw_t = (k.astype(F32) * wk_all[:, c:c + 1]).T.astype(CDT)
            upd = jnp.dot(kw_t, v_ext, preferred_element_type=F32)
            c_ref[c] = decay_all[:, c:c + 1] * c_old + upd


def _mlstm_scan(qk_l, p1, g_l, qk_c, p1c, g_c, b_if2):
    L = ML_CHUNK
    bsz, n, _ = qk_l.shape
    nc = qk_c.shape[1]
    nl = n // L
    steps = nl + 1
    vblk = BLK_D_V * LANE // BR_W
    fwd = lambda j: jnp.maximum(j - 1, 0)
    bwd = lambda j: jnp.minimum(nl - j, nl - 1)
    return pl.pallas_call(
        _mlstm_kernel,
        grid=(bsz, steps),
        in_specs=[pl.BlockSpec((1, L, 2 * BR_W), lambda b, j: (b, 0, 0)),
                  pl.BlockSpec((1, L, BR_W), lambda b, j: (b, 0, vblk)),
                  pl.BlockSpec((1, L, 2 * LANE), lambda b, j: (b, 0, 0)),
                  pl.BlockSpec((1, L, 2 * BR_W), lambda b, j: (b, fwd(j), 0)),
                  pl.BlockSpec((1, L, BR_W), lambda b, j: (b, fwd(j), vblk)),
                  pl.BlockSpec((1, L, 2 * LANE), lambda b, j: (b, fwd(j), 0)),
                  pl.BlockSpec((1, L, 2 * BR_W), lambda b, j: (b, bwd(j), 0)),
                  pl.BlockSpec((1, L, BR_W), lambda b, j: (b, bwd(j), vblk)),
                  pl.BlockSpec((1, L, 2 * LANE), lambda b, j: (b, bwd(j), 0)),
                  pl.BlockSpec((1, 2 * LANE), lambda b, j: (0, 0))],
        out_specs=[pl.BlockSpec((1, L, BR_W), lambda b, j: (b, j, 0)),
                   pl.BlockSpec((1, L, BR_W), lambda b, j: (b, steps - 1 - j, 0))],
        out_shape=[jax.ShapeDtypeStruct((bsz, n + nc, BR_W), F32),
                   jax.ShapeDtypeStruct((bsz, n + nc, BR_W), F32)],
        scratch_shapes=[pltpu.VMEM((2 * HEADS, LANE, 2 * LANE), F32),
                        pltpu.VMEM((2, 8, LANE), F32)],
        compiler_params=_cparams(("parallel", "arbitrary")),
        name="mlstm_scan",
    )(qk_c, p1c, g_c, qk_l, p1, g_l, qk_l, p1, g_l, b_if2)


def _merge_kernel(x_ref, mod_ref, ya, yb, yc, hf_ref, hb_ref, do_ref, dg_ref, gn_ref, mp_ref, wbr_ref, wout_ref,
                  gfin_ref, o_ref, *, d, final_norm):
    parts = []
    for h in range(HEADS):
        sl = slice(h * LANE, (h + 1) * LANE)
        hh = hf_ref[0, :, sl] + hb_ref[0, :, sl]
        hn = hh * lax.rsqrt(jnp.mean(hh * hh, axis=-1, keepdims=True) + EPS)
        y = hn * gn_ref[:, sl] * jax.nn.sigmoid(do_ref[0, :, sl].astype(F32))
        parts.append((y * _silu(dg_ref[0, :, sl].astype(F32))).astype(CDT))
    yd = jnp.concatenate(parts, axis=1)

    acc = None
    for i, y in enumerate((ya[...], yb[...], yc[...], yd)):
        z = jnp.dot(y, wbr_ref[i], preferred_element_type=F32)
        t = jax.nn.sigmoid(mp_ref[:, i * d:(i + 1) * d].astype(F32)) * z
        acc = t if acc is None else acc + t
    out = jnp.dot(acc.astype(CDT), wout_ref[...], preferred_element_type=F32)
    x = x_ref[...] + mod_ref[0, 2:3, :] * out
    if final_norm:
        x = x * lax.rsqrt(jnp.mean(x * x, axis=-1, keepdims=True) + EPS) * gfin_ref[...]
    o_ref[...] = x


def _merge(x2, mod, mod_row, ys, h_f, h_b, f_off, b_off, p1, gnorm, mp, w_br, w_out, g_final, final_norm):
    t, d = x2.shape
    tm = ML_CHUNK
    per = p1.shape[1] // tm
    wblk = BR_W // LANE
    y_spec = pl.BlockSpec((tm, BR_W), lambda i: (i, 0))
    return pl.pallas_call(
        functools.partial(_merge_kernel, d=d, final_norm=final_norm),
        grid=(t // tm,),
        in_specs=[pl.BlockSpec((tm, d), lambda i: (i, 0)),
                  pl.BlockSpec((1, 3, d), lambda i: (mod_row(i), 0, 0)),
                  y_spec, y_spec, y_spec,
                  pl.BlockSpec((1, tm, BR_W), lambda i: (i // per, f_off + i % per, 0)),
                  pl.BlockSpec((1, tm, BR_W), lambda i: (i // per, b_off + i % per, 0)),
                  pl.BlockSpec((1, tm, BR_W), lambda i: (i // per, i % per, BLK_D_O // wblk)),
                  pl.BlockSpec((1, tm, BR_W), lambda i: (i // per, i % per, BLK_D_GATE // wblk)),
                  pl.BlockSpec((1, BR_W), lambda i: (0, 0)),
                  pl.BlockSpec((tm, N_BRANCH * d), lambda i: (i, 0)),
                  pl.BlockSpec((N_BRANCH, BR_W, d), lambda i: (0, 0, 0), pipeline_mode=pl.Buffered(1)),
                  pl.BlockSpec((d, d), lambda i: (0, 0), pipeline_mode=pl.Buffered(1)),
                  pl.BlockSpec((1, d), lambda i: (0, 0))],
        out_specs=pl.BlockSpec((tm, d), lambda i: (i, 0)),
        out_shape=jax.ShapeDtypeStruct((t, d), F32),
        compiler_params=_cparams(("parallel",)),
        name="merge",
    )(x2, mod, *ys, h_f, h_b, p1, p1, gnorm.reshape(1, BR_W), mp, w_br, w_out, g_final.reshape(1, d))


def _split_offsets(d):
    widths = (3 * BR_W, BR_W, BR_W, BR_W, MLA_Q_LORA, MLA_KV_LORA, MLA_ROPE, BR_W,
              3 * BR_W, BR_W, 4 * HEADS, BR_W, N_BRANCH * d)
    names = ('a_qkv', 'a_gate', 'b_in', 'b_gate', 'c_q', 'c_kv', 'c_kr', 'c_gate',
             'd_qkv', 'd_o', 'd_if', 'd_gate', 'merge')
    offs = np.concatenate([[0], np.cumsum(widths)])
    return {nm: (int(offs[i]), int(offs[i + 1])) for i, nm in enumerate(names)}


def _rope_swap_perm():
    half = MLA_ROPE // 2
    q = half // 2
    return np.concatenate([np.arange(q, half), np.arange(0, q), np.arange(half + q, MLA_ROPE), np.arange(half, half + q)])


def _prep_w_in(w_in):
    nl, d, _ = w_in.shape
    o = _split_offsets(d)
    w_in = w_in.astype(CDT)
    cut = lambda nm: w_in[:, :, o[nm][0]:o[nm][1]]
    kr = cut('c_kr')
    d_if = cut('d_if')
    d_if = d_if.reshape(nl, d, 2, 2, HEADS)
    zpad = jnp.zeros((nl, d, LANE - 2 * HEADS), w_in.dtype)
    li = jnp.concatenate([d_if[:, :, :, 0, :].reshape(nl, d, 2 * HEADS), zpad], axis=-1)
    lf = jnp.concatenate([d_if[:, :, :, 1, :].reshape(nl, d, 2 * HEADS), zpad], axis=-1)
    w1 = jnp.concatenate([
        w_in[:, :, :o['c_kv'][1]], cut('c_gate'), cut('d_qkv'), cut('d_o'), cut('d_gate'),
        kr, kr[:, :, _rope_swap_perm()], jnp.zeros((nl, d, (BLK_D_LI - BLK_C_KR - 1) * LANE), w_in.dtype),
        li, lf], axis=-1)
    assert w1.shape[-1] == N1, w1.shape
    return w1.astype(CDT), cut('merge').astype(CDT)


def _prep_b_if(b_if):
    nl = b_if.shape[0]
    b = b_if.reshape(nl, 2, 2, HEADS)
    zpad = jnp.zeros((nl, LANE - 2 * HEADS), b_if.dtype)
    li = jnp.concatenate([b[:, :, 0, :].reshape(nl, 2 * HEADS), zpad], axis=-1)
    lf = jnp.concatenate([b[:, :, 1, :].reshape(nl, 2 * HEADS), zpad], axis=-1)
    return jnp.concatenate([li, lf], axis=-1).reshape(nl, 1, 2 * LANE)


def _prep_w_uq(w_uq):
    nl, r, _ = w_uq.shape
    w = w_uq.reshape(nl, r, HEADS, MLA_NOPE + MLA_ROPE)
    rope = w[..., MLA_NOPE:]
    w = jnp.concatenate([w[..., :MLA_NOPE], rope, rope[..., _rope_swap_perm()]], axis=-1)
    return w.reshape(nl, r, HEADS * 2 * LANE).astype(CDT)


def kernel(x, c, ctx, c_ctx, norm_g, w_mod, b_mod, w_in, na_rpb, pool_w, pool_scale, mla_gq, mla_gkv, w_uq, w_ukv,
           conv_w, b_if, ml_gnorm, w_br, w_out, g_final):
    bsz, n, d = x.shape
    nc = ctx.shape[1]
    depth = w_in.shape[0]
    rows = n // GRID_W
    assert n % (NA_QROWS * GRID_W) == 0 and rows >= NA_KROWS
    assert nc == ML_CHUNK and n % ML_CHUNK == 0 and nc % 8 == 0

    w1_all, w2_all = _prep_w_in(w_in)
    wq_all = _prep_w_uq(w_uq)
    wkv_all = w_ukv.astype(CDT)
    wbr_all = w_br.astype(CDT)
    wout_all = w_out.astype(CDT)
    poolw_all = pool_w.astype(CDT)
    bif_all = _prep_b_if(b_if)
    bias_all = _na_column_tables(na_rpb)
    tab_lat = _rope_tables(n, False)
    tab_ctx = _rope_tables(nc, True)

    n_cond = -(-(bsz + 1) // 8) * 8
    cond = jnp.concatenate([c, c_ctx[None, :], jnp.zeros((n_cond - bsz - 1, d), F32)], axis=0)
    mods = _modulation(cond, w_mod, b_mod)
    mods = mods.reshape(depth, n_cond, 3, d)

    tm_l = _pick(n, 1024)
    tm_c = _pick(bsz * nc, 1024)
    tn1 = 2560 if N1 % 2560 == 0 else _pick(N1, 512)
    tm_norm = _pick(n, 512)
    tn2 = _pick(N_BRANCH * d, 2048)
    lat_row = lambda i: i // (n // tm_norm)
    ctx_row = lambda i: bsz
    lat_row_m = lambda i: i // (n // ML_CHUNK)

    xl = x.reshape(bsz * n, d)
    xc = ctx.reshape(bsz * nc, d)
    for l in range(depth):
        need_ctx = l < depth - 1
        last = l == depth - 1
        mod = mods[l]
        hn_l = _mod_norm(xl, mod, lat_row, norm_g[l], tm_norm)
        hn_c = _mod_norm(xc, mod, ctx_row, norm_g[l], _pick(bsz * nc, 512))
        p1, gates = _inproj(hn_l, w1_all[l], tm_l, tn1, 2 * LANE)
        p2 = _inproj(hn_l, w2_all[l], tm_l, tn2, 0)[0]
        p1c, gates_c = _inproj(hn_c, w1_all[l], tm_c, tn1, 2 * LANE)
        p1_3 = p1.reshape(bsz, n, N1)
        p1c_3 = p1c.reshape(bsz, nc, N1)

        ya = _neighbourhood(p1_3, p1c_3, bias_all[l], bsz, n, nc)
        yb = _pool(p1_3, poolw_all[l], pool_scale[l])
        q_l, k_l, v_l = _mla_proj(p1, n, mla_gq[l], mla_gkv[l], wq_all[l], wkv_all[l], tab_lat)
        q_c, k_c, v_c = _mla_proj(p1c, nc, mla_gq[l], mla_gkv[l], wq_all[l], wkv_all[l], tab_ctx)
        r3 = lambda a, m: a.reshape(bsz, m, a.shape[-1])
        yc = _mla_attention(r3(q_l, n), r3(k_l, n), r3(v_l, n), r3(k_c, nc), r3(v_c, nc), p1_3, bsz, n, nc)
        qk_l = _short_conv(p1_3, conv_w[l])
        qk_c = _short_conv(p1c_3, conv_w[l])
        g_l = gates.reshape(bsz, n, 2 * LANE)
        g_c = gates_c.reshape(bsz, nc, 2 * LANE)
        h_f, h_b = _mlstm_scan(qk_l, p1_3, g_l, qk_c, p1c_3, g_c, bif_all[l])

        f2 = lambda a: a.reshape(-1, a.shape[-1])
        xl_new = _merge(xl, mod, lat_row_m, (f2(ya), f2(yb), f2(yc)), h_f, h_b, nc // ML_CHUNK, 0, p1_3, ml_gnorm[l],
                        p2, wbr_all[l], wout_all[l], g_final, last)
        if need_ctx:
            p2c = _inproj(hn_c, w2_all[l], tm_c, tn2, 0)[0]
            ya_c = _ctx_attention(p1c_3, BLK_A_Q, LANE, p1c_3, BLK_A_K, LANE, p1c_3, BLK_A_V, 1, p1c_3, BLK_A_GATE,
                                  HEAD_DIM ** -0.5, False)
            yb_c = _pool(p1c_3, poolw_all[l], pool_scale[l])
            yc_c = _ctx_attention(r3(q_c, nc), 0, 2 * LANE, r3(k_c, nc), 0, 2 * LANE, r3(v_c, nc), 0, 2,
                                  p1c_3, BLK_C_GATE, 1.0, True)
            xc = _merge(xc, mod, ctx_row, (f2(ya_c), f2(yb_c), f2(yc_c)), h_f, h_b, 0, n // ML_CHUNK, p1c_3, ml_gnorm[l],
                        p2c, wbr_all[l], wout_all[l], g_final, False)
        xl = xl_new
    return xl.reshape(bsz, n, d)
```
